```python
import math
import jax, jax.numpy as jnp
from jax import lax
import numpy as np

D_MODEL = 1024
BATCH = 8
SEQ = 2048
DEPTH = 4
DEC_BATCH = 128
DEC_SEQ = 4
PAST_LEN = 8192
PAGE_SIZE = 128

N_HEADS = 16
HEAD_DIM = 64
N_KV_HEADS = 4
GROUP = N_HEADS // N_KV_HEADS
QKV_DIM = (N_HEADS + 2 * N_KV_HEADS) * HEAD_DIM
Q_BLOCK = 128
WINDOW = 128
SB_BIAS_INIT = -6.0
N_BUCKETS = 32
MAX_EXACT = N_BUCKETS // 2
MAX_DISTANCE = 128
PEER_HEADS = 8
PEER_TOPK = 16
N_KEYS = 128
N_EXPERTS = N_KEYS * N_KEYS
PEER_DK = 256
TOKEN_BLOCK = 128
N_MIXERS = 2
N_SB_LAYERS = (DEPTH + 1) // 2
N_SWA_LAYERS = DEPTH // 2
DN_ALPHA = (2.0 * DEPTH) ** 0.25
DN_BETA = (8.0 * DEPTH) ** -0.25
LN_EPS = 1e-5

kernel_name = 'hybrid_stickbreak_swa_peer_decoder_step'


def layer_norm(x, g, b):
    xf = x.astype(jnp.float32)
    mu = jnp.mean(xf, -1, keepdims=True)
    var = jnp.mean(jnp.square(xf - mu), -1, keepdims=True)
    y = (xf - mu) * lax.rsqrt(var + LN_EPS) * g.astype(jnp.float32) + b.astype(jnp.float32)
    return y.astype(x.dtype)


def ada_params(c, w, b):
    return jnp.split(jax.nn.silu(c) @ w + b, 6, axis=-1)


def modulate(x, shift, scale):
    return x * (1.0 + scale[:, None, :]) + shift[:, None, :]


def post_norm(x, branch, gate, g, b):
    return layer_norm(DN_ALPHA * x + gate[:, None, :] * branch, g, b)


def project_qkv(h, w):
    B, T, _ = h.shape
    qkv = h @ w
    nq = N_HEADS * HEAD_DIM
    nk = N_KV_HEADS * HEAD_DIM
    q = qkv[..., :nq].reshape(B, T, N_KV_HEADS, GROUP, HEAD_DIM)
    k = qkv[..., nq:nq + nk].reshape(B, T, N_KV_HEADS, HEAD_DIM)
    v = qkv[..., nq + nk:].reshape(B, T, N_KV_HEADS, HEAD_DIM)
    return q, k, v


def t5_bucket(dist):
    d = jnp.maximum(dist, 0)
    large = MAX_EXACT + (jnp.log(jnp.maximum(d, 1).astype(jnp.float32) / MAX_EXACT)
                         / math.log(MAX_DISTANCE / MAX_EXACT) * (N_BUCKETS - MAX_EXACT)).astype(jnp.int32)
    large = jnp.minimum(large, N_BUCKETS - 1)
    return jnp.where(d < MAX_EXACT, d, large)


def rel_bias_from_dist(rel_bias, dist):
    b = rel_bias[t5_bucket(dist)].astype(jnp.float32)
    return jnp.transpose(b, (2, 0, 1)).reshape(N_KV_HEADS, GROUP, dist.shape[0], dist.shape[1])


def stick_breaking_weights(z, mask):
    ax = z.ndim - 1
    log_beta = jax.nn.log_sigmoid(z)
    log_rest = jnp.where(mask, jax.nn.log_sigmoid(-z), 0.0)
    later = lax.cumsum(log_rest, axis=ax, reverse=True) - log_rest
    return jnp.where(mask, jnp.exp(log_beta + later), 0.0)


def sb_prompt(q, k, v, sb_bias):
    T = q.shape[1]
    scale = HEAD_DIM ** -0.5
    bias = sb_bias.astype(jnp.float32).reshape(N_KV_HEADS, GROUP, 1, 1)
    outs = []
    for blk in range(T // Q_BLOCK):
        start, end = blk * Q_BLOCK, (blk + 1) * Q_BLOCK
        z = jnp.einsum('bqhgd,bkhd->bhgqk', q[:, start:end], k[:, :end]).astype(jnp.float32) * scale + bias
        mask = jnp.arange(end)[None, :] < jnp.arange(start, end)[:, None]
        a = stick_breaking_weights(z, mask).astype(v.dtype)
        outs.append(jnp.einsum('bhgqk,bkhd->bqhgd', a, v[:, :end]))
    return jnp.concatenate(outs, axis=1)


def sb_sample(q, k_new, v_new, k_past, v_past, sb_bias):
    T = q.shape[1]
    P = k_past.shape[1]
    scale = HEAD_DIM ** -0.5
    bias = sb_bias.astype(jnp.float32).reshape(N_KV_HEADS, GROUP, 1, 1)
    z = jnp.concatenate([jnp.einsum('bqhgd,bkhd->bhgqk', q, k_past),
                         jnp.einsum('bqhgd,bkhd->bhgqk', q, k_new)], axis=-1).astype(jnp.float32) * scale + bias
    mask = jnp.concatenate([jnp.ones((T, P), dtype=bool),
                            jnp.arange(T)[None, :] < jnp.arange(T)[:, None]], axis=1)
    a = stick_breaking_weights(z, mask).astype(v_new.dtype)
    return (jnp.einsum('bhgqk,bkhd->bqhgd', a[..., :P], v_past)
            + jnp.einsum('bhgqk,bkhd->bqhgd', a[..., P:], v_new))


def sink_softmax(logits, mask, sinks):
    s = sinks.astype(jnp.float32).reshape(N_KV_HEADS, GROUP, 1, 1)
    logits = jnp.where(mask, logits, -jnp.inf)
    m = jnp.maximum(jnp.max(logits, axis=-1, keepdims=True), s)
    p = jnp.exp(logits - m)
    return p / (jnp.sum(p, axis=-1, keepdims=True) + jnp.exp(s - m))


def swa_prompt(q, k, v, sinks, rel_bias):
    B, T = q.shape[:2]
    nb = T // WINDOW
    scale = HEAD_DIM ** -0.5
    qb = q.reshape(B, nb, WINDOW, N_KV_HEADS, GROUP, HEAD_DIM)
    kb = k.reshape(B, nb, WINDOW, N_KV_HEADS, HEAD_DIM)
    vb = v.reshape(B, nb, WINDOW, N_KV_HEADS, HEAD_DIM)
    pad = ((0, 0), (1, 0), (0, 0), (0, 0), (0, 0))
    kk = jnp.concatenate([jnp.pad(kb, pad)[:, :-1], kb], axis=2)
    vv = jnp.concatenate([jnp.pad(vb, pad)[:, :-1], vb], axis=2)
    z = jnp.einsum('bnqhgd,bnkhd->bnhgqk', qb, kk).astype(jnp.float32) * scale
    i = jnp.arange(WINDOW)[:, None]
    j = jnp.arange(2 * WINDOW)[None, :]
    dist = i + WINDOW - j
    key_pos = jnp.arange(nb)[:, None, None] * WINDOW - WINDOW + j[None]
    mask = (dist >= 0) & (dist < WINDOW) & (key_pos >= 0)
    z = z + rel_bias_from_dist(rel_bias, dist)
    p = sink_softmax(z, mask[:, None, None], sinks).astype(v.dtype)
    o = jnp.einsum('bnhgqk,bnkhd->bnqhgd', p, vv)
    return o.reshape(B, T, N_KV_HEADS, GROUP, HEAD_DIM)


def swa_sample(q, k_new, v_new, k_buf, v_buf, sinks, rel_bias, past_len):
    T = q.shape[1]
    W = k_buf.shape[1]
    scale = HEAD_DIM ** -0.5
    z = jnp.concatenate([jnp.einsum('bqhgd,bkhd->bhgqk', q, k_buf),
                         jnp.einsum('bqhgd,bkhd->bhgqk', q, k_new)], axis=-1).astype(jnp.float32) * scale
    q_pos = past_len + jnp.arange(T)
    k_pos = jnp.concatenate([past_len - W + jnp.arange(W), past_len + jnp.arange(T)])
    dist = q_pos[:, None] - k_pos[None, :]
    mask = (dist >= 0) & (dist < WINDOW)
    z = z + rel_bias_from_dist(rel_bias, dist)
    p = sink_softmax(z, mask, sinks).astype(v_new.dtype)
    return (jnp.einsum('bhgqk,bkhd->bqhgd', p[..., :W], v_buf)
            + jnp.einsum('bhgqk,bkhd->bqhgd', p[..., W:], v_new))


def peer_route(h, w_q, sub_keys):
    N = h.shape[0]
    qr = (h @ w_q).reshape(N, PEER_HEADS, 2, PEER_DK // 2)
    s = jnp.einsum('nhpd,pkd->nhpk', qr, sub_keys).astype(jnp.float32)
    top_s, top_i = lax.top_k(s, PEER_TOPK)
    cand = (top_s[:, :, 0, :, None] + top_s[:, :, 1, None, :]).reshape(N, PEER_HEADS, PEER_TOPK * PEER_TOPK)
    best_s, best_c = lax.top_k(cand, PEER_TOPK)
    i1 = jnp.take_along_axis(top_i[:, :, 0], best_c // PEER_TOPK, axis=-1)
    i2 = jnp.take_along_axis(top_i[:, :, 1], best_c % PEER_TOPK, axis=-1)
    ids = (i1 * N_KEYS + i2).astype(jnp.int32)
    gates = jax.nn.softmax(best_s, axis=-1)
    return ids.reshape(N, -1), gates.reshape(N, -1)


def peer_experts(h, ids, gates, u, v):
    N, D = h.shape
    n_pad = (-N) % TOKEN_BLOCK
    nb = (N + n_pad) // TOKEN_BLOCK
    hb = jnp.pad(h, ((0, n_pad), (0, 0))).reshape(nb, TOKEN_BLOCK, D)
    ib = jnp.pad(ids, ((0, n_pad), (0, 0))).reshape(nb, TOKEN_BLOCK, -1)
    gb = jnp.pad(gates, ((0, n_pad), (0, 0))).reshape(nb, TOKEN_BLOCK, -1)

    def block(args):
        hx, ix, gx = args
        act = jax.nn.gelu(jnp.einsum('td,tkd->tk', hx, u[ix]), approximate=False)
        return jnp.einsum('tk,tkd->td', gx.astype(hx.dtype) * act, v[ix])

    return lax.map(block, (hb, ib, gb)).reshape(nb * TOKEN_BLOCK, D)[:N]


def peer_ffn(h, w_q, sub_keys, u, v):
    lead = h.shape[:-1]
    hf = h.reshape(-1, h.shape[-1])
    ids, gates = peer_route(hf, w_q, sub_keys)
    return peer_experts(hf, ids, gates, u, v).reshape(*lead, h.shape[-1])


def setup_inputs(seed: int = 0) -> dict:
    key = jax.random.key(seed)
    ks = jax.random.split(key, 24)
    n_pages = PAST_LEN // PAGE_SIZE
    n_used = DEC_BATCH * n_pages
    n_phys = n_used + (n_used + 3) // 4
    swa_buf = min(WINDOW, PAST_LEN)

    def nrm(k, shape, s=1.0):
        return jax.random.normal(k, shape, jnp.float32) * s

    x_prompt = nrm(ks[0], (BATCH, SEQ, D_MODEL))
    x_sample = nrm(ks[1], (DEC_BATCH, DEC_SEQ, D_MODEL))
    c_prompt = nrm(ks[2], (BATCH, D_MODEL))
    c_sample = nrm(ks[3], (DEC_BATCH, D_MODEL))
    cache_sb_k = nrm(ks[4], (N_SB_LAYERS, n_phys, PAGE_SIZE, N_KV_HEADS, HEAD_DIM))
    cache_sb_v = nrm(ks[5], (N_SB_LAYERS, n_phys, PAGE_SIZE, N_KV_HEADS, HEAD_DIM), DN_BETA)
    page_table = jax.random.permutation(ks[6], n_phys)[:n_used].reshape(DEC_BATCH, n_pages).astype(jnp.int32)
    cache_swa_k = nrm(ks[7], (N_SWA_LAYERS, DEC_BATCH, swa_buf, N_KV_HEADS, HEAD_DIM))
    cache_swa_v = nrm(ks[8], (N_SWA_LAYERS, DEC_BATCH, swa_buf, N_KV_HEADS, HEAD_DIM), DN_BETA)
    w_ada = nrm(ks[9], (DEPTH, D_MODEL, 6 * D_MODEL), 0.5 * D_MODEL ** -0.5)
    b_ada = nrm(ks[10], (DEPTH, 6 * D_MODEL), 0.01)
    v_cols = jnp.concatenate([jnp.ones(((N_HEADS + N_KV_HEADS) * HEAD_DIM,), jnp.float32),
                              jnp.full((N_KV_HEADS * HEAD_DIM,), DN_BETA, jnp.float32)])
    w_qkv = nrm(ks[11], (DEPTH, D_MODEL, QKV_DIM), D_MODEL ** -0.5) * v_cols
    w_o = nrm(ks[12], (DEPTH, N_HEADS * HEAD_DIM, D_MODEL), DN_BETA * (N_HEADS * HEAD_DIM) ** -0.5)
    attn_sinks = nrm(ks[13], (N_SWA_LAYERS, N_HEADS))
    sb_logit_bias = SB_BIAS_INIT + nrm(ks[23], (N_SB_LAYERS, N_HEADS), 0.1)
    rel_bias = nrm(ks[14], (N_BUCKETS, N_HEADS), 0.5)
    ln_mix_g = 1.0 + nrm(ks[15], (DEPTH, D_MODEL), 0.02)
    ln_mix_b = nrm(ks[16], (DEPTH, D_MODEL), 0.02)
    ln_ffn_g = 1.0 + nrm(ks[17], (DEPTH, D_MODEL), 0.02)
    ln_ffn_b = nrm(ks[18], (DEPTH, D_MODEL), 0.02)
    w_peer_q = nrm(ks[19], (DEPTH, D_MODEL, PEER_HEADS * PEER_DK), D_MODEL ** -0.5)
    peer_sub_keys = nrm(ks[20], (DEPTH, 2, N_KEYS, PEER_DK // 2), (PEER_DK // 2) ** -0.5)
    peer_u = nrm(ks[21], (DEPTH, N_EXPERTS, D_MODEL), D_MODEL ** -0.5)
    peer_v = nrm(ks[22], (DEPTH, N_EXPERTS, D_MODEL), DN_BETA)
    return {'x_prompt': x_prompt, 'x_sample': x_sample, 'c_prompt': c_prompt, 'c_sample': c_sample,
            'cache_sb_k': cache_sb_k, 'cache_sb_v': cache_sb_v, 'page_table': page_table,
            'cache_swa_k': cache_swa_k, 'cache_swa_v': cache_swa_v,
            'w_ada': w_ada, 'b_ada': b_ada, 'w_qkv': w_qkv, 'w_o': w_o,
            'attn_sinks': attn_sinks, 'sb_logit_bias': sb_logit_bias, 'rel_bias': rel_bias,
            'ln_mix_g': ln_mix_g, 'ln_mix_b': ln_mix_b, 'ln_ffn_g': ln_ffn_g, 'ln_ffn_b': ln_ffn_b,
            'w_peer_q': w_peer_q, 'peer_sub_keys': peer_sub_keys, 'peer_u': peer_u, 'peer_v': peer_v}


def reference(x_prompt, x_sample, c_prompt, c_sample, cache_sb_k, cache_sb_v, page_table,
              cache_swa_k, cache_swa_v, w_ada, b_ada, w_qkv, w_o, attn_sinks, sb_logit_bias, rel_bias,
              ln_mix_g, ln_mix_b, ln_ffn_g, ln_ffn_b, w_peer_q, peer_sub_keys, peer_u, peer_v):
    B, T, D = x_prompt.shape
    DB, TS, _ = x_sample.shape
    n_pages = page_table.shape[1]
    past_len = n_pages * cache_sb_k.shape[2]
    swa_buf = cache_swa_k.shape[2]
    keep_p = min(WINDOW, T)
    xp, xs = x_prompt, x_sample
    sb_kp, sb_vp, sb_ks, sb_vs = [], [], [], []
    swa_kp, swa_vp, swa_ks, swa_vs = [], [], [], []
    for layer in range(DEPTH):
        ap = ada_params(c_prompt, w_ada[layer], b_ada[layer])
        asm = ada_params(c_sample, w_ada[layer], b_ada[layer])
        qp, kp, vp = project_qkv(modulate(xp, ap[0], ap[1]), w_qkv[layer])
        qs, kn, vn = project_qkv(modulate(xs, asm[0], asm[1]), w_qkv[layer])
        slot = layer // N_MIXERS
        if layer % N_MIXERS == 0:
            k_past = cache_sb_k[slot][page_table].reshape(DB, past_len, N_KV_HEADS, HEAD_DIM)
            v_past = cache_sb_v[slot][page_table].reshape(DB, past_len, N_KV_HEADS, HEAD_DIM)
            op = sb_prompt(qp, kp, vp, sb_logit_bias[slot])
            osm = sb_sample(qs, kn, vn, k_past, v_past, sb_logit_bias[slot])
            sb_kp.append(kp)
            sb_vp.append(vp)
            sb_ks.append(kn)
            sb_vs.append(vn)
        else:
            op = swa_prompt(qp, kp, vp, attn_sinks[slot], rel_bias)
            osm = swa_sample(qs, kn, vn, cache_swa_k[slot], cache_swa_v[slot],
                             attn_sinks[slot], rel_bias, past_len)
            swa_kp.append(kp[:, T - keep_p:])
            swa_vp.append(vp[:, T - keep_p:])
            swa_ks.append(jnp.concatenate([cache_swa_k[slot], kn], axis=1)[:, TS:TS + swa_buf])
            swa_vs.append(jnp.concatenate([cache_swa_v[slot], vn], axis=1)[:, TS:TS + swa_buf])
        xp = post_norm(xp, op.reshape(B, T, -1) @ w_o[layer], ap[2], ln_mix_g[layer], ln_mix_b[layer])
        xs = post_norm(xs, osm.reshape(DB, TS, -1) @ w_o[layer], asm[2], ln_mix_g[layer], ln_mix_b[layer])
        fp = peer_ffn(modulate(xp, ap[3], ap[4]), w_peer_q[layer], peer_sub_keys[layer], peer_u[layer], peer_v[layer])
        fs = peer_ffn(modulate(xs, asm[3], asm[4]), w_peer_q[layer], peer_sub_keys[layer], peer_u[layer], peer_v[layer])
        xp = post_norm(xp, fp, ap[5], ln_ffn_g[layer], ln_ffn_b[layer])
        xs = post_norm(xs, fs, asm[5], ln_ffn_g[layer], ln_ffn_b[layer])
    return (xp, xs,
            jnp.stack(sb_kp), jnp.stack(sb_vp), jnp.stack(sb_ks), jnp.stack(sb_vs),
            jnp.stack(swa_kp), jnp.stack(swa_vp), jnp.stack(swa_ks), jnp.stack(swa_vs))
```

```python
import functools
import math

import jax
import jax.numpy as jnp
from jax import lax
from jax.experimental import pallas as pl
from jax.experimental.pallas import tpu as pltpu

F32 = jnp.float32
BF16 = jnp.bfloat16
NEG_INF = float("-inf")

N_HEADS = 16
HEAD_DIM = 64
N_KV_HEADS = 4
GROUP = N_HEADS // N_KV_HEADS
Q_BLOCK = 128
WINDOW = 128
N_BUCKETS = 32
MAX_EXACT = N_BUCKETS // 2
MAX_DISTANCE = 128
PEER_HEADS = 8
PEER_TOPK = 16
N_KEYS = 128
LN_EPS = 1e-5

V7X_VMEM_BYTES = 64 * 1024 * 1024
LANES = 128
SUBLANES = 8

NT_DIMS = (((1,), (1,)), ((), ()))
TN_DIMS = (((0,), (0,)), ((), ()))


def _vmem_limit(block_bytes):
    return int(min(V7X_VMEM_BYTES * 7 // 8, 2 * block_bytes + 24 * 1024 * 1024))


def _params(semantics, block_bytes):
    return pltpu.CompilerParams(dimension_semantics=semantics, vmem_limit_bytes=_vmem_limit(block_bytes))


def _nbytes(shape, dtype):
    return math.prod(shape) * jnp.dtype(dtype).itemsize


def _ada_kernel(c_ref, w_ref, b_ref, o_ref):
    c = c_ref[...]
    s = c * jax.nn.sigmoid(c)
    o_ref[0] = jnp.dot(s, w_ref[0], precision=lax.Precision.HIGHEST, preferred_element_type=F32) + b_ref[0]


def _ada_params(c_all, w_ada, b_ada):
    depth, d, d6 = w_ada.shape
    rows = c_all.shape[0]
    tn = d6 // 6
    return pl.pallas_call(
        _ada_kernel,
        grid=(depth, d6 // tn),
        in_specs=[
            pl.BlockSpec((rows, d), lambda l, j: (0, 0)),
            pl.BlockSpec((1, d, tn), lambda l, j: (l, 0, j)),
            pl.BlockSpec((1, 1, tn), lambda l, j: (l, 0, j)),
        ],
        out_specs=pl.BlockSpec((1, rows, tn), lambda l, j: (l, 0, j)),
        out_shape=jax.ShapeDtypeStruct((depth, rows, d6), F32),
        compiler_params=_params(("parallel", "parallel"), _nbytes((d, tn), F32) + 2 * _nbytes((rows, d), F32)),
        name="ada_params",
    )(c_all, w_ada, b_ada.reshape(depth, 1, d6))


def _mod_spec(tm, tb, d, chunk):
    if tm == 1:
        return pl.BlockSpec((1, 1, d), lambda b, i, *_: (b, 0, chunk))
    return pl.BlockSpec((1, tb, d), lambda b, i, *_: (b, i, chunk))


def _qkv_kernel(x_ref, shift_ref, scale_ref, w_ref, q_ref, k_ref, v_ref, *, nq, nk):
    h = x_ref[0] * (1.0 + scale_ref[0]) + shift_ref[0]
    qkv = jnp.dot(h.astype(BF16), w_ref[...], preferred_element_type=F32)
    q_ref[0] = (qkv[:, :nq] * (HEAD_DIM ** -0.5)).astype(BF16)
    k_ref[0] = qkv[:, nq:nq + nk]
    v_ref[0] = qkv[:, nq + nk:]


def _qkv_proj(x, ada, w_bf, tb):
    bsz, t, d = x.shape
    tm = ada.shape[1]
    nq = N_HEADS * HEAD_DIM
    nk = N_KV_HEADS * HEAD_DIM
    blk = lambda n: pl.BlockSpec((1, tb, n), lambda b, i: (b, i, 0))
    return pl.pallas_call(
        functools.partial(_qkv_kernel, nq=nq, nk=nk),
        grid=(bsz, t // tb),
        in_specs=[blk(d), _mod_spec(tm, tb, d, 0), _mod_spec(tm, tb, d, 1),
                  pl.BlockSpec(w_bf.shape, lambda b, i: (0, 0))],
        out_specs=[blk(nq), blk(nk), blk(nk)],
        out_shape=[jax.ShapeDtypeStruct((bsz, t, nq), BF16),
                   jax.ShapeDtypeStruct((bsz, t, nk), F32),
                   jax.ShapeDtypeStruct((bsz, t, nk), F32)],
        compiler_params=_params(("parallel", "parallel"),
                                _nbytes(w_bf.shape, BF16) + 4 * _nbytes((tb, d), F32) + _nbytes((tb, nq + 2 * nk), F32)),
        name="qkv_proj",
    )(x, ada, ada, w_bf)


def _log_sigmoid_pair(z):
    t = jnp.log1p(jnp.exp(-jnp.abs(z)))
    return jnp.minimum(z, 0.0) - t, -jnp.maximum(z, 0.0) - t


def _split_bf16(x):
    hi = x.astype(BF16)
    lo = (x - hi.astype(F32)).astype(BF16)
    return hi, lo


def _sb_prompt_kernel(bias_ref, q_ref, k_ref, v_ref, o_ref):
    qb = pl.program_id(1)
    blk = Q_BLOCK
    rows = GROUP * blk
    r_i = lax.broadcasted_iota(jnp.int32, (2 * blk, 2 * blk), 0)
    c_i = lax.broadcasted_iota(jnp.int32, (2 * blk, 2 * blk), 1)
    suffix = jnp.where((c_i >= blk) | ((r_i % blk) > c_i), 1.0, 0.0).astype(BF16)
    t_idx = lax.broadcasted_iota(jnp.int32, (rows, blk), 0) % blk
    s_idx = lax.broadcasted_iota(jnp.int32, (rows, blk), 1)
    diag_mask = s_idx < t_idx
    for g in range(N_KV_HEADS):
        heads = [g * GROUP + hh for hh in range(GROUP)]
        qg = jnp.concatenate([q_ref[0, :, h * HEAD_DIM:(h + 1) * HEAD_DIM] for h in heads], axis=0)
        bias = jnp.concatenate([jnp.full((blk, blk), bias_ref[h], F32) for h in heads], axis=0)

        def body(i, carry, g=g, qg=qg, bias=bias):
            o, c = carry
            start = pl.multiple_of((qb - i) * blk, blk)
            kj = k_ref[0, pl.ds(start, blk), g * HEAD_DIM:(g + 1) * HEAD_DIM].astype(BF16)
            vj = v_ref[0, pl.ds(start, blk), g * HEAD_DIM:(g + 1) * HEAD_DIM].astype(BF16)
            z = lax.dot_general(qg, kj, NT_DIMS, preferred_element_type=F32) + bias
            mask = diag_mask | (i > 0)
            log_beta, log_rest = _log_sigmoid_pair(z)
            log_rest = jnp.where(mask, log_rest, 0.0)
            hi, lo = _split_bf16(log_rest)
            sums = jnp.dot(jnp.concatenate([hi, lo], axis=1), suffix, preferred_element_type=F32)
            a = jnp.where(mask, jnp.exp(log_beta + sums[:, :blk] + c), 0.0)
            o = o + jnp.dot(a.astype(BF16), vj, preferred_element_type=F32)
            return o, c + sums[:, blk:]

        o, _ = lax.fori_loop(0, qb + 1, body, (jnp.zeros((rows, HEAD_DIM), F32), jnp.zeros((rows, blk), F32)))
        for hh, h in enumerate(heads):
            o_ref[0, :, h * HEAD_DIM:(h + 1) * HEAD_DIM] = o[hh * blk:(hh + 1) * blk].astype(o_ref.dtype)


def _sb_prompt(q, k, v, sb_bias):
    bsz, t, nq = q.shape
    nk = k.shape[-1]
    return pl.pallas_call(
        _sb_prompt_kernel,
        grid=(bsz, t // Q_BLOCK),
        in_specs=[pl.BlockSpec(memory_space=pltpu.SMEM),
                  pl.BlockSpec((1, Q_BLOCK, nq), lambda b, i: (b, i, 0)),
                  pl.BlockSpec((1, t, nk), lambda b, i: (b, 0, 0)),
                  pl.BlockSpec((1, t, nk), lambda b, i: (b, 0, 0))],
        out_specs=pl.BlockSpec((1, Q_BLOCK, nq), lambda b, i: (b, i, 0)),
        out_shape=jax.ShapeDtypeStruct((bsz, t, nq), BF16),
        compiler_params=_params(("parallel", "arbitrary"), 2 * _nbytes((t, nk), F32)),
        name="sb_prompt",
    )(sb_bias.astype(F32), q, k, v)


SB_PAGES_PER_STEP = 8


def _sb_sample_kernel(pt_ref, bias_ref, qbd_ref, kn_ref, vn_ref, *refs, n_pg, ts):
    del pt_ref
    k_refs, v_refs = refs[:n_pg], refs[n_pg:2 * n_pg]
    o_ref, acc_ref, c_ref = refs[2 * n_pg:]
    i = pl.program_id(1)
    qbd = qbd_ref[0]
    bias = bias_ref[...]
    cols = qbd.shape[1]
    page = k_refs[0].shape[2]

    @pl.when(i == 0)
    def _():
        nr = kn_ref.shape[1]
        z = jnp.dot(kn_ref[0].astype(BF16), qbd, preferred_element_type=F32) + bias
        s_idx = lax.broadcasted_iota(jnp.int32, (nr, cols), 0)
        t_idx = lax.broadcasted_iota(jnp.int32, (nr, cols), 1) % ts
        mask = s_idx < t_idx
        log_beta, log_rest = _log_sigmoid_pair(z)
        log_rest = jnp.where(mask, log_rest, 0.0)
        later_m = (lax.broadcasted_iota(jnp.int32, (nr, nr), 1) > lax.broadcasted_iota(jnp.int32, (nr, nr), 0))
        later = jnp.dot(later_m.astype(F32), log_rest, precision=lax.Precision.HIGHEST,
                        preferred_element_type=F32)
        a = jnp.where(mask, jnp.exp(log_beta + later), 0.0)
        acc_ref[...] = lax.dot_general(a.astype(BF16), vn_ref[0].astype(BF16), TN_DIMS,
                                       preferred_element_type=F32)
        c_ref[...] = jnp.sum(log_rest, axis=0, keepdims=True)

    r_i = lax.broadcasted_iota(jnp.int32, (page, 2 * page), 0)
    c_i = lax.broadcasted_iota(jnp.int32, (page, 2 * page), 1)
    later_hl = jnp.where((c_i % page) > r_i, 1.0, 0.0).astype(BF16)
    acc = acc_ref[...]
    c = c_ref[...]
    for p in range(n_pg):
        kp = k_refs[p][0, 0].astype(BF16)
        vp = v_refs[p][0, 0].astype(BF16)
        z = jnp.dot(kp, qbd, preferred_element_type=F32) + bias
        log_beta, log_rest = _log_sigmoid_pair(z)
        hi, lo = _split_bf16(log_rest)
        later = jnp.dot(later_hl, jnp.concatenate([hi, lo], axis=0), preferred_element_type=F32)
        a = jnp.exp(log_beta + later + c)
        acc = acc + lax.dot_general(a.astype(BF16), vp, TN_DIMS, preferred_element_type=F32)
        c = c + jnp.sum(log_rest, axis=0, keepdims=True)
    acc_ref[...] = acc
    c_ref[...] = c

    @pl.when(i == pl.num_programs(1) - 1)
    def _():
        o_ref[0] = acc


def _block_diag_queries(q):
    db, ts, _ = q.shape
    qr = q.reshape(db, ts, N_KV_HEADS, GROUP, HEAD_DIM)
    qr = jnp.transpose(qr, (0, 2, 4, 3, 1)).reshape(db, N_KV_HEADS, HEAD_DIM, GROUP * ts)
    eye = jnp.eye(N_KV_HEADS, dtype=q.dtype)
    qbd = qr[:, :, :, None, :] * eye[None, :, None, :, None]
    return qbd.reshape(db, N_KV_HEADS * HEAD_DIM, N_KV_HEADS * GROUP * ts)


def _diag_heads(o, ts):
    db = o.shape[0]
    o6 = o.reshape(db, N_KV_HEADS, GROUP, ts, N_KV_HEADS, HEAD_DIM)
    od = jnp.stack([o6[:, g, :, :, g, :] for g in range(N_KV_HEADS)], axis=1)
    return jnp.transpose(od, (0, 3, 1, 2, 4)).reshape(db, ts, N_HEADS * HEAD_DIM)


def _col_vector(per_head, ts):
    return jnp.repeat(per_head.astype(F32), ts).reshape(1, N_HEADS * ts)


def _pad_rows(x, rows):
    return jnp.pad(x, ((0, 0), (0, rows - x.shape[1]), (0, 0)))


def _sb_sample(q, k_new, v_new, cache_k, cache_v, slot, page_table, sb_bias):
    db, ts, _ = q.shape
    n_pages = page_table.shape[1]
    page = cache_k.shape[2]
    nk = N_KV_HEADS * HEAD_DIM
    n_pg = SB_PAGES_PER_STEP
    cols = N_HEADS * ts
    nr = max(SUBLANES, ts)
    ck = cache_k.reshape(cache_k.shape[0], cache_k.shape[1], page, nk)
    cv = cache_v.reshape(cache_v.shape[0], cache_v.shape[1], page, nk)

    def page_spec(p):
        return pl.BlockSpec((1, 1, page, nk),
                            lambda b, i, pt: (slot, pt[b, n_pages - 1 - (i * n_pg + p)], 0, 0))

    grid_spec = pltpu.PrefetchScalarGridSpec(
        num_scalar_prefetch=1,
        grid=(db, n_pages // n_pg),
        in_specs=[pl.BlockSpec((1, cols), lambda b, i, pt: (0, 0)),
                  pl.BlockSpec((1, nk, cols), lambda b, i, pt: (b, 0, 0)),
                  pl.BlockSpec((1, nr, nk), lambda b, i, pt: (b, 0, 0)),
                  pl.BlockSpec((1, nr, nk), lambda b, i, pt: (b, 0, 0))]
                 + [page_spec(p) for p in range(n_pg)] + [page_spec(p) for p in range(n_pg)],
        out_specs=pl.BlockSpec((1, cols, nk), lambda b, i, pt: (b, 0, 0)),
        scratch_shapes=[pltpu.VMEM((cols, nk), F32), pltpu.VMEM((1, cols), F32)],
    )
    o = pl.pallas_call(
        functools.partial(_sb_sample_kernel, n_pg=n_pg, ts=ts),
        grid_spec=grid_spec,
        out_shape=jax.ShapeDtypeStruct((db, cols, nk), F32),
        compiler_params=_params(("parallel", "arbitrary"), 2 * n_pg * _nbytes((page, nk), F32)),
        name="sb_sample",
    )(page_table, _col_vector(sb_bias, ts), _block_diag_queries(q), _pad_rows(k_new, nr), _pad_rows(v_new, nr),
      *([ck] * n_pg), *([cv] * n_pg))
    return _diag_heads(o, ts).astype(BF16)


def _t5_bucket(dist):
    d = jnp.maximum(dist, 0)
    large = MAX_EXACT + (jnp.log(jnp.maximum(d, 1).astype(F32) / MAX_EXACT)
                         / math.log(MAX_DISTANCE / MAX_EXACT) * (N_BUCKETS - MAX_EXACT)).astype(jnp.int32)
    large = jnp.minimum(large, N_BUCKETS - 1)
    return jnp.where(d < MAX_EXACT, d, large)


def _swa_prompt_kernel(sink_ref, q_ref, kp_ref, kc_ref, vp_ref, vc_ref, rb_ref, o_ref):
    n = pl.program_id(1)
    w = WINDOW
    rows = GROUP * w
    i_idx = lax.broadcasted_iota(jnp.int32, (rows, 2 * w), 0) % w
    j_idx = lax.broadcasted_iota(jnp.int32, (rows, 2 * w), 1)
    dist = i_idx + w - j_idx
    mask = (dist >= 0) & (dist < w) & ((j_idx >= w) | (n > 0))
    for g in range(N_KV_HEADS):
        heads = [g * GROUP + hh for hh in range(GROUP)]
        sl = slice(g * HEAD_DIM, (g + 1) * HEAD_DIM)
        qg = jnp.concatenate([q_ref[0, :, h * HEAD_DIM:(h + 1) * HEAD_DIM] for h in heads], axis=0)
        kk = jnp.concatenate([kp_ref[0, :, sl], kc_ref[0, :, sl]], axis=0).astype(BF16)
        vv = jnp.concatenate([vp_ref[0, :, sl], vc_ref[0, :, sl]], axis=0).astype(BF16)
        rb = jnp.concatenate([rb_ref[h] for h in heads], axis=0)
        sink = jnp.concatenate([jnp.full((w, 1), sink_ref[h], F32) for h in heads], axis=0)
        z = lax.dot_general(qg, kk, NT_DIMS, preferred_element_type=F32) + rb
        z = jnp.where(mask, z, NEG_INF)
        m = jnp.maximum(jnp.max(z, axis=1, keepdims=True), sink)
        p = jnp.exp(z - m)
        denom = jnp.sum(p, axis=1, keepdims=True) + jnp.exp(sink - m)
        o = jnp.dot(p.astype(BF16), vv, preferred_element_type=F32) / denom
        for hh, h in enumerate(heads):
            o_ref[0, :, h * HEAD_DIM:(h + 1) * HEAD_DIM] = o[hh * w:(hh + 1) * w].astype(o_ref.dtype)


def _swa_prompt(q, k, v, sinks, rel_bias):
    bsz, t, nq = q.shape
    nk = k.shape[-1]
    w = WINDOW
    dist = jnp.arange(w)[:, None] + w - jnp.arange(2 * w)[None, :]
    rb = jnp.transpose(rel_bias[_t5_bucket(dist)].astype(F32), (2, 0, 1))
    cur = lambda n: pl.BlockSpec((1, w, n), lambda b, i: (b, i, 0))
    prev = lambda n: pl.BlockSpec((1, w, n), lambda b, i: (b, jnp.maximum(i - 1, 0), 0))
    return pl.pallas_call(
        _swa_prompt_kernel,
        grid=(bsz, t // w),
        in_specs=[pl.BlockSpec(memory_space=pltpu.SMEM), cur(nq), prev(nk), cur(nk), prev(nk), cur(nk),
                  pl.BlockSpec((N_HEADS, w, 2 * w), lambda b, i: (0, 0, 0))],
        out_specs=cur(nq),
        out_shape=jax.ShapeDtypeStruct((bsz, t, nq), BF16),
        compiler_params=_params(("parallel", "arbitrary"), _nbytes((N_HEADS, w, 2 * w), F32)),
        name="swa_prompt",
    )(sinks.astype(F32), q, k, k, v, v, rb)


SWA_SAMPLE_ROWS_PER_STEP = 8


def _swa_sample_kernel(sink_ref, qbd_ref, kn_ref, vn_ref, kb_ref, vb_ref, rbb_ref, rbn_ref, o_ref, *, nb, ts):
    cols = qbd_ref.shape[2]
    wbuf = kb_ref.shape[1]
    nr = kn_ref.shape[1]
    sink = sink_ref[...]
    t_b = lax.broadcasted_iota(jnp.int32, (wbuf, cols), 1) % ts
    j_b = lax.broadcasted_iota(jnp.int32, (wbuf, cols), 0)
    dist_b = t_b + wbuf - j_b
    mask_b = (dist_b >= 0) & (dist_b < WINDOW)
    t_n = lax.broadcasted_iota(jnp.int32, (nr, cols), 1) % ts
    s_n = lax.broadcasted_iota(jnp.int32, (nr, cols), 0)
    mask_n = (t_n - s_n >= 0) & (s_n < ts)
    for r in range(nb):
        qbd = qbd_ref[r]
        zb = jnp.dot(kb_ref[r].astype(BF16), qbd, preferred_element_type=F32) + rbb_ref[...]
        zn = jnp.dot(kn_ref[r].astype(BF16), qbd, preferred_element_type=F32) + rbn_ref[...]
        zb = jnp.where(mask_b, zb, NEG_INF)
        zn = jnp.where(mask_n, zn, NEG_INF)
        m = jnp.maximum(jnp.maximum(jnp.max(zb, axis=0, keepdims=True), jnp.max(zn, axis=0, keepdims=True)), sink)
        pb = jnp.exp(zb - m)
        pn = jnp.exp(zn - m)
        denom = jnp.sum(pb, axis=0, keepdims=True) + jnp.sum(pn, axis=0, keepdims=True) + jnp.exp(sink - m)
        inv = 1.0 / denom
        o = (lax.dot_general((pb * inv).astype(BF16), vb_ref[r].astype(BF16), TN_DIMS, preferred_element_type=F32)
             + lax.dot_general((pn * inv).astype(BF16), vn_ref[r].astype(BF16), TN_DIMS, preferred_element_type=F32))
        o_ref[r] = o


def _swa_sample(q, k_new, v_new, k_buf, v_buf, sinks, rel_bias, past_len):
    db, ts, _ = q.shape
    wbuf = k_buf.shape[1]
    nk = N_KV_HEADS * HEAD_DIM
    cols = N_HEADS * ts
    nr = max(SUBLANES, ts)
    nb = SWA_SAMPLE_ROWS_PER_STEP
    q_pos = past_len + jnp.arange(ts)
    k_pos = jnp.concatenate([past_len - wbuf + jnp.arange(wbuf), past_len + jnp.arange(nr)])
    dist = q_pos[:, None] - k_pos[None, :]
    rb = rel_bias[_t5_bucket(dist)].astype(F32)
    rb = jnp.transpose(rb, (1, 2, 0)).reshape(wbuf + nr, cols)
    blk = lambda r, c: pl.BlockSpec((nb, r, c), lambda b: (b, 0, 0))
    full = lambda r, c: pl.BlockSpec((r, c), lambda b: (0, 0))
    o = pl.pallas_call(
        functools.partial(_swa_sample_kernel, nb=nb, ts=ts),
        grid=(db // nb,),
        in_specs=[full(1, cols), blk(nk, cols), blk(nr, nk), blk(nr, nk), blk(wbuf, nk), blk(wbuf, nk),
                  full(wbuf, cols), full(nr, cols)],
        out_specs=blk(cols, nk),
        out_shape=jax.ShapeDtypeStruct((db, cols, nk), F32),
        compiler_params=_params(("parallel",), 2 * nb * _nbytes((wbuf, nk), F32)),
        name="swa_sample",
    )(_col_vector(sinks, ts), _block_diag_queries(q), _pad_rows(k_new, nr), _pad_rows(v_new, nr),
      k_buf.reshape(db, wbuf, nk), v_buf.reshape(db, wbuf, nk), rb[:wbuf], rb[wbuf:])
    return _diag_heads(o, ts).astype(BF16)


def _post_norm(x, branch, gate, g, b, alpha):
    y = alpha * x + gate * branch
    mu = jnp.mean(y, axis=-1, keepdims=True)
    yc = y - mu
    var = jnp.mean(yc * yc, axis=-1, keepdims=True)
    return yc * lax.rsqrt(var + LN_EPS) * g + b


def _wo_norm_kernel(o_ref, x_ref, gate_ref, w_ref, g_ref, b_ref, y_ref, *, alpha):
    branch = jnp.dot(o_ref[0], w_ref[...], preferred_element_type=F32)
    y_ref[0] = _post_norm(x_ref[0], branch, gate_ref[0], g_ref[...], b_ref[...], alpha)


def _wo_norm(o, x, ada, w_bf, ln_g, ln_b, alpha, tb):
    bsz, t, d = x.shape
    tm = ada.shape[1]
    nq = o.shape[-1]
    blk = lambda n: pl.BlockSpec((1, tb, n), lambda b, i: (b, i, 0))
    vec = pl.BlockSpec((1, d), lambda b, i: (0, 0))
    return pl.pallas_call(
        functools.partial(_wo_norm_kernel, alpha=alpha),
        grid=(bsz, t // tb),
        in_specs=[blk(nq), blk(d), _mod_spec(tm, tb, d, 2), pl.BlockSpec(w_bf.shape, lambda b, i: (0, 0)), vec, vec],
        out_specs=blk(d),
        out_shape=jax.ShapeDtypeStruct((bsz, t, d), F32),
        compiler_params=_params(("parallel", "parallel"), _nbytes(w_bf.shape, BF16) + 5 * _nbytes((tb, d), F32)),
        name="wo_norm",
    )(o, x, ada, w_bf, ln_g.reshape(1, d), ln_b.reshape(1, d))


def _topk_ranked(s, iota_k):
    n_keys = s.shape[0]
    pos = jnp.full(s.shape, float(PEER_TOPK), F32)
    vals = []
    for a in range(PEER_TOPK):
        m = jnp.max(s, axis=0, keepdims=True)
        idx = jnp.min(jnp.where(s == m, iota_k, float(n_keys)), axis=0, keepdims=True)
        hit = iota_k == idx
        pos = jnp.where(hit, float(a), pos)
        s = jnp.where(hit, NEG_INF, s)
        vals.append(m)
    return jnp.concatenate(vals, axis=0), pos


def _candidate_slabs():
    k = PEER_TOPK
    slabs = [("b", 0, 0, 8), ("b", 0, 8, 8)]
    for b in range(1, 8):
        slabs.append(("b", b, 0, k // (b + 1)))
    slabs.append(("a0", None, 0, 8))
    return slabs


def _peer_route_kernel(x_ref, shift_ref, scale_ref, wq_ref, sk_ref, p2_ref, e2_ref, nt_ref, wt_ref, qr_ref):
    k = PEER_TOPK
    tb = x_ref.shape[1]
    h = (x_ref[0] * (1.0 + scale_ref[0]) + shift_ref[0]).astype(BF16)
    qr_ref[...] = lax.dot_general(wq_ref[...], h, NT_DIMS, preferred_element_type=F32)
    iota_k = lax.broadcasted_iota(jnp.int32, (N_KEYS, tb), 0).astype(F32)
    row8 = lax.broadcasted_iota(jnp.int32, (SUBLANES, tb), 0).astype(F32)
    slabs = _candidate_slabs()
    flat = jnp.concatenate(
        [((row8 + a0) * k + b) if kind == "b" else (row8 + 8) for kind, b, a0, _ in slabs], axis=0)
    valid = jnp.concatenate([row8 < nv for _, _, _, nv in slabs], axis=0)
    n_cand = flat.shape[0]
    half = N_KEYS

    def head_body(hd, carry):
        base = pl.multiple_of(hd * 2 * half, 2 * half)
        q1 = qr_ref[pl.ds(base, half), :].astype(BF16)
        q2 = qr_ref[pl.ds(base + half, half), :].astype(BF16)
        s1 = jnp.dot(sk_ref[0], q1, preferred_element_type=F32)
        s2 = jnp.dot(sk_ref[1], q2, preferred_element_type=F32)
        t1, pos1 = _topk_ranked(s1, iota_k)
        t2, pos2 = _topk_ranked(s2, iota_k)
        pieces = []
        for kind, b, a0, _ in slabs:
            if kind == "b":
                pieces.append(t1[a0:a0 + 8] + t2[b:b + 1])
            else:
                pieces.append(t1[0:1] + t2[8:16])
        cand = jnp.where(valid, jnp.concatenate(pieces, axis=0), NEG_INF)
        top = t1[0:1] + t2[0:1]
        e = jnp.exp(cand - top)
        sel = jnp.zeros(cand.shape, F32)
        c = cand
        for _ in range(k):
            m = jnp.max(c, axis=0, keepdims=True)
            idx = jnp.min(jnp.where(c == m, flat, float(k * k)), axis=0, keepdims=True)
            hit = flat == idx
            sel = jnp.where(hit, 1.0, sel)
            c = jnp.where(hit, NEG_INF, c)
        z = jnp.sum(sel * e, axis=0, keepdims=True)
        n_lo = sel[0:8]
        for si in range(2, 2 + 7):
            n_lo = n_lo + sel[si * 8:(si + 1) * 8]
        last = jnp.sum(sel[n_cand - 8:], axis=0, keepdims=True)
        n_lo = n_lo + jnp.where(row8 == 0.0, last, 0.0)
        counts = jnp.concatenate([n_lo, sel[8:16]], axis=0)
        nt = jnp.zeros((N_KEYS, tb), F32)
        for a in range(k):
            nt = jnp.where(pos1 == float(a), counts[a:a + 1], nt)
        p2_ref[hd] = pos2.astype(p2_ref.dtype)
        e2_ref[hd] = jnp.exp(s2 - t2[0:1]).astype(e2_ref.dtype)
        nt_ref[hd] = nt
        wt_ref[hd] = jnp.exp(s1 - t1[0:1]) / z
        return carry

    lax.fori_loop(0, PEER_HEADS, head_body, 0)


def _peer_route(x, ada, wq_t, sub_keys_bf, tb):
    bsz, t, d = x.shape
    tm = ada.shape[1]
    n_tok = bsz * t
    nblk = t // tb
    tok = lambda b, i: b * nblk + i
    hk = pl.BlockSpec((PEER_HEADS, N_KEYS, tb), lambda b, i: (0, 0, tok(b, i)))
    return pl.pallas_call(
        _peer_route_kernel,
        grid=(bsz, nblk),
        in_specs=[pl.BlockSpec((1, tb, d), lambda b, i: (b, i, 0)), _mod_spec(tm, tb, d, 3), _mod_spec(tm, tb, d, 4),
                  pl.BlockSpec(wq_t.shape, lambda b, i: (0, 0)),
                  pl.BlockSpec(sub_keys_bf.shape, lambda b, i: (0, 0, 0))],
        out_specs=[hk, hk, hk, hk],
        out_shape=[jax.ShapeDtypeStruct((PEER_HEADS, N_KEYS, n_tok), BF16),
                   jax.ShapeDtypeStruct((PEER_HEADS, N_KEYS, n_tok), BF16),
                   jax.ShapeDtypeStruct((PEER_HEADS, N_KEYS, n_tok), F32),
                   jax.ShapeDtypeStruct((PEER_HEADS, N_KEYS, n_tok), F32)],
        scratch_shapes=[pltpu.VMEM((wq_t.shape[0], tb), F32)],
        compiler_params=_params(("parallel", "parallel"),
                                _nbytes(wq_t.shape, BF16) + 2 * _nbytes((wq_t.shape[0], tb), F32)
                                + 6 * _nbytes((PEER_HEADS, N_KEYS, tb), F32)),
        name="peer_route",
    )(x, ada, ada, wq_t, sub_keys_bf)


def _gelu_exact(x):
    return 0.5 * x * (1.0 + lax.erf(x * (2.0 ** -0.5)))


def _peer_dense_kernel(x_ref, shift_ref, scale_ref, gate_ref, g_ref, b_ref, p2_ref, e2_ref, nt_ref, wt_ref,
                       u_ref, vt_ref, y_ref, h_ref, acc_ref, *, alpha, n_chunks):
    e = pl.program_id(2)

    @pl.when(e == 0)
    def _():
        h_ref[...] = (x_ref[0] * (1.0 + scale_ref[0]) + shift_ref[0]).astype(BF16)
        acc_ref[...] = jnp.zeros_like(acc_ref)

    h = h_ref[...]
    parts = []
    for c in range(n_chunks):
        act = lax.dot_general(u_ref[c * N_KEYS:(c + 1) * N_KEYS, :], h, NT_DIMS, preferred_element_type=F32)
        gates = None
        for hd in range(PEER_HEADS):
            n_sel = nt_ref[hd, pl.ds(e * n_chunks + c, 1), :].astype(BF16)
            w = wt_ref[hd, pl.ds(e * n_chunks + c, 1), :].astype(BF16)
            term = jnp.where(p2_ref[hd] < n_sel, e2_ref[hd], jnp.zeros((), BF16)) * w
            gates = term if gates is None else gates + term
        parts.append(_gelu_exact(act).astype(BF16) * gates)
    p = parts[0] if n_chunks == 1 else jnp.concatenate(parts, axis=0)
    acc_ref[...] += jnp.dot(vt_ref[...], p, preferred_element_type=F32)

    @pl.when(e == pl.num_programs(2) - 1)
    def _():
        branch = acc_ref[...].T
        y_ref[0] = _post_norm(x_ref[0], branch, gate_ref[0], g_ref[...], b_ref[...], alpha)


def _peer_dense(x, ada, route, u_bf, vt_bf, ln_g, ln_b, alpha, tb, te):
    bsz, t, d = x.shape
    tm = ada.shape[1]
    nblk = t // tb
    n_exp = u_bf.shape[0]
    n_chunks = te // N_KEYS
    p2, e2, nt, wt = route
    tok = lambda b, i: b * nblk + i
    hk = pl.BlockSpec((PEER_HEADS, N_KEYS, tb), lambda b, i, e: (0, 0, tok(b, i)))
    xblk = pl.BlockSpec((1, tb, d), lambda b, i, e: (b, i, 0))
    vec = pl.BlockSpec((1, d), lambda b, i, e: (0, 0))
    return pl.pallas_call(
        functools.partial(_peer_dense_kernel, alpha=alpha, n_chunks=n_chunks),
        grid=(bsz, nblk, n_exp // te),
        in_specs=[xblk, _mod_spec(tm, tb, d, 3), _mod_spec(tm, tb, d, 4), _mod_spec(tm, tb, d, 5), vec, vec,
                  hk, hk, hk, hk,
                  pl.BlockSpec((te, d), lambda b, i, e: (e, 0)),
                  pl.BlockSpec((d, te), lambda b, i, e: (0, e))],
        out_specs=xblk,
        out_shape=jax.ShapeDtypeStruct((bsz, t, d), F32),
        scratch_shapes=[pltpu.VMEM((tb, d), BF16), pltpu.VMEM((d, tb), F32)],
        compiler_params=_params(("parallel", "parallel", "arbitrary"),
                                2 * _nbytes((te, d), BF16) + 4 * _nbytes((tb, d), F32)
                                + 2 * _nbytes((PEER_HEADS, N_KEYS, tb), F32)),
        name="peer_dense",
    )(x, ada, ada, ada, ln_g.reshape(1, d), ln_b.reshape(1, d), p2, e2, nt, wt, u_bf, vt_bf)


def _token_block(t, want):
    tb = min(t, want)
    while t % tb:
        tb //= 2
    return tb


TOKEN_BLOCK_PROJ = 512
TOKEN_BLOCK_ROUTE = 256
TOKEN_BLOCK_DENSE = 512
EXPERT_TILE = 256


def kernel(x_prompt, x_sample, c_prompt, c_sample, cache_sb_k, cache_sb_v, page_table, cache_swa_k, cache_swa_v,
           w_ada, b_ada, w_qkv, w_o, attn_sinks, sb_logit_bias, rel_bias, ln_mix_g, ln_mix_b, ln_ffn_g, ln_ffn_b,
           w_peer_q, peer_sub_keys, peer_u, peer_v):
    bsz, t, d = x_prompt.shape
    db, ts, _ = x_sample.shape
    depth = w_ada.shape[0]
    n_pages = page_table.shape[1]
    past_len = n_pages * cache_sb_k.shape[2]
    swa_buf = cache_swa_k.shape[2]
    keep_p = min(WINDOW, t)
    alpha = (2.0 * depth) ** 0.25
    nk = N_KV_HEADS * HEAD_DIM
    n_s = db * ts

    ada_all = _ada_params(jnp.concatenate([c_prompt, c_sample], axis=0), w_ada, b_ada)
    xp = x_prompt
    xs = x_sample.reshape(1, n_s, d)

    tb_p = _token_block(t, TOKEN_BLOCK_PROJ)
    tb_s = _token_block(n_s, TOKEN_BLOCK_PROJ)
    tr_p = _token_block(t, TOKEN_BLOCK_ROUTE)
    tr_s = _token_block(n_s, TOKEN_BLOCK_ROUTE)
    td_p = _token_block(t, TOKEN_BLOCK_DENSE)
    td_s = _token_block(n_s, TOKEN_BLOCK_DENSE)

    sb_kp, sb_vp, sb_ks, sb_vs = [], [], [], []
    swa_kp, swa_vp, swa_ks, swa_vs = [], [], [], []
    for layer in range(depth):
        ada_p = ada_all[layer, :bsz].reshape(bsz, 1, 6 * d)
        ada_s = jnp.repeat(ada_all[layer, bsz:], ts, axis=0).reshape(1, n_s, 6 * d)
        w_qkv_bf = w_qkv[layer].astype(BF16)
        w_o_bf = w_o[layer].astype(BF16)
        wq_t = w_peer_q[layer].T.astype(BF16)
        sub_keys_bf = peer_sub_keys[layer].astype(BF16)
        u_bf = peer_u[layer].astype(BF16)
        vt_bf = peer_v[layer].T.astype(BF16)

        qp, kp, vp = _qkv_proj(xp, ada_p, w_qkv_bf, tb_p)
        qs, kn, vn = _qkv_proj(xs, ada_s, w_qkv_bf, tb_s)
        qs3 = qs.reshape(db, ts, N_HEADS * HEAD_DIM)
        kn3 = kn.reshape(db, ts, nk)
        vn3 = vn.reshape(db, ts, nk)
        slot = layer // 2
        if layer % 2 == 0:
            op = _sb_prompt(qp, kp, vp, sb_logit_bias[slot])
            osm = _sb_sample(qs3, kn3, vn3, cache_sb_k, cache_sb_v, slot, page_table, sb_logit_bias[slot])
            sb_kp.append(kp)
            sb_vp.append(vp)
            sb_ks.append(kn3)
            sb_vs.append(vn3)
        else:
            op = _swa_prompt(qp, kp, vp, attn_sinks[slot], rel_bias)
            osm = _swa_sample(qs3, kn3, vn3, cache_swa_k[slot], cache_swa_v[slot], attn_sinks[slot], rel_bias,
                              past_len)
            swa_kp.append(kp[:, t - keep_p:])
            swa_vp.append(vp[:, t - keep_p:])
            ck = cache_swa_k[slot].reshape(db, swa_buf, nk)
            cv = cache_swa_v[slot].reshape(db, swa_buf, nk)
            swa_ks.append(jnp.concatenate([ck, kn3], axis=1)[:, ts:ts + swa_buf])
            swa_vs.append(jnp.concatenate([cv, vn3], axis=1)[:, ts:ts + swa_buf])
        xp = _wo_norm(op, xp, ada_p, w_o_bf, ln_mix_g[layer], ln_mix_b[layer], alpha, tb_p)
        xs = _wo_norm(osm.reshape(1, n_s, -1), xs, ada_s, w_o_bf, ln_mix_g[layer], ln_mix_b[layer], alpha, tb_s)

        route_p = _peer_route(xp, ada_p, wq_t, sub_keys_bf, tr_p)
        route_s = _peer_route(xs, ada_s, wq_t, sub_keys_bf, tr_s)
        xp = _peer_dense(xp, ada_p, route_p, u_bf, vt_bf, ln_ffn_g[layer], ln_ffn_b[layer], alpha, td_p, EXPERT_TILE)
        xs = _peer_dense(xs, ada_s, route_s, u_bf, vt_bf, ln_ffn_g[layer], ln_ffn_b[layer], alpha, td_s, EXPERT_TILE)

    def heads(xs_list, lead):
        return jnp.stack(xs_list).reshape(len(xs_list), *lead, N_KV_HEADS, HEAD_DIM)

    return (xp, xs.reshape(db, ts, d),
            heads(sb_kp, (bsz, t)), heads(sb_vp, (bsz, t)), heads(sb_ks, (db, ts)), heads(sb_vs, (db, ts)),
            heads(swa_kp, (bsz, keep_p)), heads(swa_vp, (bsz, keep_p)),
            heads(swa_ks, (db, swa_buf)), heads(swa_vs, (db, swa_buf)))
```

```python
import functools
import math

import jax
import jax.numpy as jnp
from jax import lax
from jax.experimental import pallas as pl
from jax.experimental.pallas import tpu as pltpu

F32 = jnp.float32
BF16 = jnp.bfloat16
NEG_INF = float("-inf")

N_HEADS = 16
HEAD_DIM = 64
N_KV_HEADS = 4
GROUP = N_HEADS // N_KV_HEADS
Q_BLOCK = 128
WINDOW = 128
N_BUCKETS = 32
MAX_EXACT = N_BUCKETS // 2
MAX_DISTANCE = 128
PEER_HEADS = 8
PEER_TOPK = 16
N_KEYS = 128
LN_EPS = 1e-5

V7X_VMEM_BYTES = 64 * 1024 * 1024
LANES = 128
SUBLANES = 8

NT_DIMS = (((1,), (1,)), ((), ()))
TN_DIMS = (((0,), (0,)), ((), ()))


def _vmem_limit(block_bytes):
    return int(min(V7X_VMEM_BYTES * 7 // 8, 2 * block_bytes + 24 * 1024 * 1024))


def _params(semantics, block_bytes):
    return pltpu.CompilerParams(dimension_semantics=semantics, vmem_limit_bytes=_vmem_limit(block_bytes))


def _nbytes(shape, dtype):
    return math.prod(shape) * jnp.dtype(dtype).itemsize


def _ada_kernel(c_ref, w_ref, b_ref, o_ref):
    c = c_ref[...]
    s = c * jax.nn.sigmoid(c)
    o_ref[0] = jnp.dot(s, w_ref[0], precision=lax.Precision.HIGHEST, preferred_element_type=F32) + b_ref[0]


def _ada_params(c_all, w_ada, b_ada):
    depth, d, d6 = w_ada.shape
    rows = c_all.shape[0]
    tn = d6 // 6
    return pl.pallas_call(
        _ada_kernel,
        grid=(depth, d6 // tn),
        in_specs=[
            pl.BlockSpec((rows, d), lambda l, j: (0, 0)),
            pl.BlockSpec((1, d, tn), lambda l, j: (l, 0, j)),
            pl.BlockSpec((1, 1, tn), lambda l, j: (l, 0, j)),
        ],
        out_specs=pl.BlockSpec((1, rows, tn), lambda l, j: (l, 0, j)),
        out_shape=jax.ShapeDtypeStruct((depth, rows, d6), F32),
        compiler_params=_params(("parallel", "parallel"), _nbytes((d, tn), F32) + 2 * _nbytes((rows, d), F32)),
        name="ada_params",
    )(c_all, w_ada, b_ada.reshape(depth, 1, d6))


def _mod_spec(tm, tb, d, chunk):
    if tm == 1:
        return pl.BlockSpec((1, 1, d), lambda b, i, *_: (b, 0, chunk))
    return pl.BlockSpec((1, tb, d), lambda b, i, *_: (b, i, chunk))


def _qkv_kernel(x_ref, shift_ref, scale_ref, w_ref, q_ref, k_ref, v_ref, *vt_refs, nq, nk):
    h = x_ref[0] * (1.0 + scale_ref[0]) + shift_ref[0]
    qkv = jnp.dot(h.astype(BF16), w_ref[...], preferred_element_type=F32)
    q_ref[0] = (qkv[:, :nq] * (HEAD_DIM ** -0.5)).astype(BF16)
    k_ref[0] = qkv[:, nq:nq + nk]
    v = qkv[:, nq + nk:]
    v_ref[0] = v
    if vt_refs:
        vt = v.T
        for n in range(vt.shape[1] // Q_BLOCK):
            vt_refs[0][0, n] = vt[:, n * Q_BLOCK:(n + 1) * Q_BLOCK].astype(BF16)


def _qkv_proj(x, ada, w_bf, tb, value_pages=False):
    bsz, t, d = x.shape
    tm = ada.shape[1]
    nq = N_HEADS * HEAD_DIM
    nk = N_KV_HEADS * HEAD_DIM
    blk = lambda n: pl.BlockSpec((1, tb, n), lambda b, i: (b, i, 0))
    out_specs = [blk(nq), blk(nk), blk(nk)]
    out_shape = [jax.ShapeDtypeStruct((bsz, t, nq), BF16),
                 jax.ShapeDtypeStruct((bsz, t, nk), F32),
                 jax.ShapeDtypeStruct((bsz, t, nk), F32)]
    if value_pages:
        out_specs.append(pl.BlockSpec((1, tb // Q_BLOCK, nk, Q_BLOCK), lambda b, i: (b, i, 0, 0)))
        out_shape.append(jax.ShapeDtypeStruct((bsz, t // Q_BLOCK, nk, Q_BLOCK), BF16))
    return pl.pallas_call(
        functools.partial(_qkv_kernel, nq=nq, nk=nk),
        grid=(bsz, t // tb),
        in_specs=[blk(d), _mod_spec(tm, tb, d, 0), _mod_spec(tm, tb, d, 1),
                  pl.BlockSpec(w_bf.shape, lambda b, i: (0, 0))],
        out_specs=out_specs,
        out_shape=out_shape,
        compiler_params=_params(("parallel", "parallel"),
                                _nbytes(w_bf.shape, BF16) + 4 * _nbytes((tb, d), F32) + _nbytes((tb, nq + 2 * nk), F32)),
        name="qkv_proj",
    )(x, ada, ada, w_bf)


def _log_sigmoid_pair(z):
    t = jnp.log(1.0 + jnp.exp(-jnp.abs(z)))
    return jnp.minimum(z, 0.0) - t, -jnp.maximum(z, 0.0) - t


def _split_bf16(x):
    hi = x.astype(BF16)
    lo = (x - hi.astype(F32)).astype(BF16)
    return hi, lo


def _sb_prompt_kernel(bias_ref, q_ref, k_ref, vt_ref, o_ref):
    qb = pl.program_id(1)
    blk = Q_BLOCK
    cols = GROUP * blk
    ncols = N_KV_HEADS * cols
    nk = N_KV_HEADS * HEAD_DIM
    r_i = lax.broadcasted_iota(jnp.int32, (blk, 2 * blk), 0)
    c_i = lax.broadcasted_iota(jnp.int32, (blk, 2 * blk), 1)
    later_hl = jnp.where((c_i % blk) > r_i, 1.0, 0.0).astype(BF16)
    s_idx = lax.broadcasted_iota(jnp.int32, (blk, ncols), 0)
    t_idx = lax.broadcasted_iota(jnp.int32, (blk, ncols), 1) % blk
    diag_mask = s_idx < t_idx
    q_rows = []
    for g in range(N_KV_HEADS):
        qg = jnp.concatenate([q_ref[0, :, h * HEAD_DIM:(h + 1) * HEAD_DIM]
                              for h in range(g * GROUP, (g + 1) * GROUP)], axis=0)
        pieces = []
        if g:
            pieces.append(jnp.zeros((cols, g * HEAD_DIM), BF16))
        pieces.append(qg)
        if g + 1 < N_KV_HEADS:
            pieces.append(jnp.zeros((cols, nk - (g + 1) * HEAD_DIM), BF16))
        q_rows.append(jnp.concatenate(pieces, axis=1))
    q_all = jnp.concatenate(q_rows, axis=0)
    bias = jnp.concatenate([jnp.full((1, blk), bias_ref[h], F32) for h in range(N_HEADS)], axis=1)

    def block(j, carry, diagonal):
        o_ts, c = carry
        start = pl.multiple_of(j * blk, blk)
        kj = k_ref[0, pl.ds(start, blk), :].astype(BF16)
        z = lax.dot_general(kj, q_all, NT_DIMS, preferred_element_type=F32) + bias
        log_beta, log_rest = _log_sigmoid_pair(z)
        if diagonal:
            log_rest = jnp.where(diag_mask, log_rest, 0.0)
        hi, lo = _split_bf16(log_rest)
        later = jnp.dot(later_hl, jnp.concatenate([hi, lo], axis=0), preferred_element_type=F32)
        a = jnp.exp(log_beta + later + c)
        if diagonal:
            a = jnp.where(diag_mask, a, 0.0)
        a = a.astype(BF16)
        o_ts = tuple(
            o_ts[g] + jnp.dot(vt_ref[0, j, g * HEAD_DIM:(g + 1) * HEAD_DIM, :], a[:, g * cols:(g + 1) * cols],
                              preferred_element_type=F32)
            for g in range(N_KV_HEADS))
        return o_ts, c + jnp.sum(log_rest, axis=0, keepdims=True)

    zero = (tuple(jnp.zeros((HEAD_DIM, cols), F32) for _ in range(N_KV_HEADS)), jnp.zeros((1, ncols), F32))
    carry = block(qb, zero, True)
    o_ts, _ = lax.fori_loop(0, qb, lambda i, cr: block(qb - 1 - i, cr, False), carry)
    for h in range(N_HEADS):
        g, hh = divmod(h, GROUP)
        o_ref[0, h * HEAD_DIM:(h + 1) * HEAD_DIM, :] = o_ts[g][:, hh * blk:(hh + 1) * blk].astype(o_ref.dtype)


def _sb_prompt(q, k, vt_pages, sb_bias):
    bsz, t, nq = q.shape
    nk = k.shape[-1]
    return pl.pallas_call(
        _sb_prompt_kernel,
        grid=(bsz, t // Q_BLOCK),
        in_specs=[pl.BlockSpec(memory_space=pltpu.SMEM),
                  pl.BlockSpec((1, Q_BLOCK, nq), lambda b, i: (b, i, 0)),
                  pl.BlockSpec((1, t, nk), lambda b, i: (b, 0, 0)),
                  pl.BlockSpec((1, t // Q_BLOCK, nk, Q_BLOCK), lambda b, i: (b, 0, 0, 0))],
        out_specs=pl.BlockSpec((1, nq, Q_BLOCK), lambda b, i: (b, 0, i)),
        out_shape=jax.ShapeDtypeStruct((bsz, nq, t), BF16),
        compiler_params=_params(("parallel", "arbitrary"), 2 * _nbytes((t, nk), F32)),
        name="sb_prompt",
    )(sb_bias.astype(F32), q, k, vt_pages)


SB_PAGES_PER_STEP = 16


def _sb_sample_kernel(pt_ref, bias_ref, suffix_ref, q_ref, kn_ref, vn_ref, *refs, n_pg, ts):
    del pt_ref
    k_refs, v_refs = refs[:n_pg], refs[n_pg:2 * n_pg]
    o_ref, acc_ref, c_ref = refs[2 * n_pg:]
    i = pl.program_id(1)
    q = q_ref[0]
    bias = bias_ref[...]
    rows = q.shape[0]
    page = kn_ref.shape[2]

    def pair_update(acc, c, log_beta, log_rest, vt2, mask=None):
        hi, lo = _split_bf16(log_rest)
        sums = jnp.dot(jnp.concatenate([hi, lo], axis=1), suffix_ref[...], preferred_element_type=F32)
        a = jnp.exp(log_beta + sums[:, :2 * page] + c)
        if mask is not None:
            a = jnp.where(mask, a, 0.0)
        acc = acc + lax.dot_general(a.astype(BF16), vt2, NT_DIMS, preferred_element_type=F32)
        return acc, c + sums[:, 2 * page:]

    @pl.when(i == 0)
    def _():
        zeros = jnp.zeros((kn_ref.shape[1], page), BF16)
        kt2 = jnp.concatenate([zeros, kn_ref[0].astype(BF16)], axis=1)
        vt2 = jnp.concatenate([zeros, vn_ref[0].astype(BF16)], axis=1)
        z = jnp.dot(q, kt2, preferred_element_type=F32) + bias
        s_idx = lax.broadcasted_iota(jnp.int32, (rows, 2 * page), 1) - page
        t_idx = lax.broadcasted_iota(jnp.int32, (rows, 2 * page), 0) % ts
        mask = (s_idx >= 0) & (s_idx < t_idx)
        log_beta, log_rest = _log_sigmoid_pair(z)
        log_rest = jnp.where(mask, log_rest, 0.0)
        acc, c = pair_update(jnp.zeros(acc_ref.shape, F32), jnp.zeros(c_ref.shape, F32), log_beta, log_rest, vt2,
                             mask)
        acc_ref[...] = acc
        c_ref[...] = c

    n_pairs = n_pg // 2
    kt_all = jnp.concatenate([r[0, 0] for r in k_refs], axis=1).astype(BF16)
    vt_all = jnp.concatenate([r[0, 0] for r in v_refs], axis=1).astype(BF16)
    z = jnp.dot(q, kt_all, preferred_element_type=F32) + jnp.concatenate([bias] * n_pairs, axis=1)
    log_beta, log_rest = _log_sigmoid_pair(z)
    hi, lo = _split_bf16(log_rest)
    pair = lambda x, p: x[:, 2 * page * p:2 * page * (p + 1)]
    stacked = jnp.concatenate([jnp.concatenate([pair(hi, p), pair(lo, p)], axis=1) for p in range(n_pairs)], axis=0)
    sums = jnp.dot(stacked, suffix_ref[...], preferred_element_type=F32)
    c = c_ref[...]
    a_parts = []
    for p in range(n_pairs):
        sums_p = sums[p * rows:(p + 1) * rows]
        a_parts.append(jnp.exp(pair(log_beta, p) + sums_p[:, :2 * page] + c).astype(BF16))
        c = c + sums_p[:, 2 * page:]
    a_all = jnp.concatenate(a_parts, axis=1)
    acc = acc_ref[...] + lax.dot_general(a_all, vt_all, NT_DIMS, preferred_element_type=F32)
    acc_ref[...] = acc
    c_ref[...] = c

    @pl.when(i == pl.num_programs(1) - 1)
    def _():
        o_ref[0] = acc


def _block_diag_queries(q):
    db, ts, _ = q.shape
    qr = q.reshape(db, ts, N_KV_HEADS, GROUP, HEAD_DIM)
    qr = jnp.transpose(qr, (0, 2, 4, 3, 1)).reshape(db, N_KV_HEADS, HEAD_DIM, GROUP * ts)
    eye = jnp.eye(N_KV_HEADS, dtype=q.dtype)
    qbd = qr[:, :, :, None, :] * eye[None, :, None, :, None]
    return qbd.reshape(db, N_KV_HEADS * HEAD_DIM, N_KV_HEADS * GROUP * ts)


def _diag_heads(o, ts):
    db = o.shape[0]
    o6 = o.reshape(db, N_KV_HEADS, GROUP, ts, N_KV_HEADS, HEAD_DIM)
    od = jnp.stack([o6[:, g, :, :, g, :] for g in range(N_KV_HEADS)], axis=1)
    return jnp.transpose(od, (0, 3, 1, 2, 4)).reshape(db, ts, N_HEADS * HEAD_DIM)


def _col_vector(per_head, ts):
    return jnp.repeat(per_head.astype(F32), ts).reshape(1, N_HEADS * ts)


def _pad_rows(x, rows):
    return jnp.pad(x, ((0, 0), (0, rows - x.shape[1]), (0, 0)))


def _pair_suffix_matrix(page):
    lane = jnp.arange(2 * page)
    pos = jnp.where(lane < page, lane + page, lane - page)
    later = (pos[:, None] > pos[None, :]).astype(BF16)
    both = jnp.concatenate([later, jnp.ones((2 * page, 2 * page), BF16)], axis=1)
    return jnp.concatenate([both, both], axis=0)


def _keys_on_lanes(x, lanes):
    xt = jnp.swapaxes(x, 1, 2)
    return jnp.pad(xt, ((0, 0), (0, 0), (0, lanes - xt.shape[2])))


def _sb_sample(q, k_new, v_new, cache_k, cache_v, slot, page_table, sb_bias):
    db, ts, _ = q.shape
    n_pages = page_table.shape[1]
    n_layers, n_phys, page = cache_k.shape[:3]
    nk = N_KV_HEADS * HEAD_DIM
    n_pg = SB_PAGES_PER_STEP
    rows = N_HEADS * ts
    ckt = jnp.transpose(cache_k, (0, 1, 3, 4, 2)).reshape(n_layers, n_phys, nk, page)
    cvt = jnp.transpose(cache_v, (0, 1, 3, 4, 2)).reshape(n_layers, n_phys, nk, page)
    bias = jnp.broadcast_to(_col_vector(sb_bias, ts).reshape(rows, 1), (rows, 2 * page))

    def page_spec(p):
        return pl.BlockSpec((1, 1, nk, page),
                            lambda b, i, pt: (slot, pt[b, n_pages - 1 - (i * n_pg + p)], 0, 0))

    grid_spec = pltpu.PrefetchScalarGridSpec(
        num_scalar_prefetch=1,
        grid=(db, n_pages // n_pg),
        in_specs=[pl.BlockSpec((rows, 2 * page), lambda b, i, pt: (0, 0)),
                  pl.BlockSpec((4 * page, 4 * page), lambda b, i, pt: (0, 0)),
                  pl.BlockSpec((1, rows, nk), lambda b, i, pt: (b, 0, 0)),
                  pl.BlockSpec((1, nk, page), lambda b, i, pt: (b, 0, 0)),
                  pl.BlockSpec((1, nk, page), lambda b, i, pt: (b, 0, 0))]
                 + [page_spec(p) for p in range(n_pg)] + [page_spec(p) for p in range(n_pg)],
        out_specs=pl.BlockSpec((1, rows, nk), lambda b, i, pt: (b, 0, 0)),
        scratch_shapes=[pltpu.VMEM((rows, nk), F32), pltpu.VMEM((rows, 2 * page), F32)],
    )
    o = pl.pallas_call(
        functools.partial(_sb_sample_kernel, n_pg=n_pg, ts=ts),
        grid_spec=grid_spec,
        out_shape=jax.ShapeDtypeStruct((db, rows, nk), F32),
        compiler_params=_params(("parallel", "arbitrary"), 2 * n_pg * _nbytes((page, nk), F32)),
        name="sb_sample",
    )(page_table, bias, _pair_suffix_matrix(page), jnp.swapaxes(_block_diag_queries(q), 1, 2),
      _keys_on_lanes(k_new, page), _keys_on_lanes(v_new, page), *([ckt] * n_pg), *([cvt] * n_pg))
    return _diag_heads(o, ts).astype(BF16)


def _t5_bucket(dist):
    d = jnp.maximum(dist, 0)
    large = MAX_EXACT + (jnp.log(jnp.maximum(d, 1).astype(F32) / MAX_EXACT)
                         / math.log(MAX_DISTANCE / MAX_EXACT) * (N_BUCKETS - MAX_EXACT)).astype(jnp.int32)
    large = jnp.minimum(large, N_BUCKETS - 1)
    return jnp.where(d < MAX_EXACT, d, large)


def _swa_prompt_kernel(sink_ref, q_ref, kp_ref, kc_ref, vp_ref, vc_ref, rb_ref, o_ref):
    n = pl.program_id(1)
    w = WINDOW
    rows = GROUP * w
    i_idx = lax.broadcasted_iota(jnp.int32, (rows, 2 * w), 0) % w
    j_idx = lax.broadcasted_iota(jnp.int32, (rows, 2 * w), 1)
    dist = i_idx + w - j_idx
    mask = (dist >= 0) & (dist < w) & ((j_idx >= w) | (n > 0))
    for g in range(N_KV_HEADS):
        heads = [g * GROUP + hh for hh in range(GROUP)]
        sl = slice(g * HEAD_DIM, (g + 1) * HEAD_DIM)
        qg = jnp.concatenate([q_ref[0, :, h * HEAD_DIM:(h + 1) * HEAD_DIM] for h in heads], axis=0)
        kk = jnp.concatenate([kp_ref[0, :, sl], kc_ref[0, :, sl]], axis=0).astype(BF16)
        vv = jnp.concatenate([vp_ref[0, :, sl], vc_ref[0, :, sl]], axis=0).astype(BF16)
        rb = jnp.concatenate([rb_ref[h] for h in heads], axis=0)
        sink = jnp.concatenate([jnp.full((w, 1), sink_ref[h], F32) for h in heads], axis=0)
        z = lax.dot_general(qg, kk, NT_DIMS, preferred_element_type=F32) + rb
        z = jnp.where(mask, z, NEG_INF)
        m = jnp.maximum(jnp.max(z, axis=1, keepdims=True), sink)
        p = jnp.exp(z - m)
        denom = jnp.sum(p, axis=1, keepdims=True) + jnp.exp(sink - m)
        o = jnp.dot(p.astype(BF16), vv, preferred_element_type=F32) / denom
        for hh, h in enumerate(heads):
            o_ref[0, :, h * HEAD_DIM:(h + 1) * HEAD_DIM] = o[hh * w:(hh + 1) * w].astype(o_ref.dtype)


def _swa_prompt(q, k, v, sinks, rel_bias):
    bsz, t, nq = q.shape
    nk = k.shape[-1]
    w = WINDOW
    dist = jnp.arange(w)[:, None] + w - jnp.arange(2 * w)[None, :]
    rb = jnp.transpose(rel_bias[_t5_bucket(dist)].astype(F32), (2, 0, 1))
    cur = lambda n: pl.BlockSpec((1, w, n), lambda b, i: (b, i, 0))
    prev = lambda n: pl.BlockSpec((1, w, n), lambda b, i: (b, jnp.maximum(i - 1, 0), 0))
    return pl.pallas_call(
        _swa_prompt_kernel,
        grid=(bsz, t // w),
        in_specs=[pl.BlockSpec(memory_space=pltpu.SMEM), cur(nq), prev(nk), cur(nk), prev(nk), cur(nk),
                  pl.BlockSpec((N_HEADS, w, 2 * w), lambda b, i: (0, 0, 0))],
        out_specs=cur(nq),
        out_shape=jax.ShapeDtypeStruct((bsz, t, nq), BF16),
        compiler_params=_params(("parallel", "arbitrary"), _nbytes((N_HEADS, w, 2 * w), F32)),
        name="swa_prompt",
    )(sinks.astype(F32), q, k, k, v, v, rb)


SWA_SAMPLE_ROWS_PER_STEP = 8


def _swa_sample_kernel(sink_ref, qbd_ref, kn_ref, vn_ref, kb_ref, vb_ref, rbb_ref, rbn_ref, o_ref, *, nb, ts):
    cols = qbd_ref.shape[2]
    wbuf = kb_ref.shape[1]
    nr = kn_ref.shape[1]
    sink = sink_ref[...]
    t_b = lax.broadcasted_iota(jnp.int32, (wbuf, cols), 1) % ts
    j_b = lax.broadcasted_iota(jnp.int32, (wbuf, cols), 0)
    dist_b = t_b + wbuf - j_b
    mask_b = (dist_b >= 0) & (dist_b < WINDOW)
    t_n = lax.broadcasted_iota(jnp.int32, (nr, cols), 1) % ts
    s_n = lax.broadcasted_iota(jnp.int32, (nr, cols), 0)
    mask_n = (t_n - s_n >= 0) & (s_n < ts)
    for r in range(nb):
        qbd = qbd_ref[r]
        zb = jnp.dot(kb_ref[r].astype(BF16), qbd, preferred_element_type=F32) + rbb_ref[...]
        zn = jnp.dot(kn_ref[r].astype(BF16), qbd, preferred_element_type=F32) + rbn_ref[...]
        zb = jnp.where(mask_b, zb, NEG_INF)
        zn = jnp.where(mask_n, zn, NEG_INF)
        m = jnp.maximum(jnp.maximum(jnp.max(zb, axis=0, keepdims=True), jnp.max(zn, axis=0, keepdims=True)), sink)
        pb = jnp.exp(zb - m)
        pn = jnp.exp(zn - m)
        denom = jnp.sum(pb, axis=0, keepdims=True) + jnp.sum(pn, axis=0, keepdims=True) + jnp.exp(sink - m)
        inv = 1.0 / denom
        o = (lax.dot_general((pb * inv).astype(BF16), vb_ref[r].astype(BF16), TN_DIMS, preferred_element_type=F32)
             + lax.dot_general((pn * inv).astype(BF16), vn_ref[r].astype(BF16), TN_DIMS, preferred_element_type=F32))
        o_ref[r] = o


def _swa_sample(q, k_new, v_new, k_buf, v_buf, sinks, rel_bias, past_len):
    db, ts, _ = q.shape
    wbuf = k_buf.shape[1]
    nk = N_KV_HEADS * HEAD_DIM
    cols = N_HEADS * ts
    nr = max(SUBLANES, ts)
    nb = SWA_SAMPLE_ROWS_PER_STEP
    q_pos = past_len + jnp.arange(ts)
    k_pos = jnp.concatenate([past_len - wbuf + jnp.arange(wbuf), past_len + jnp.arange(nr)])
    dist = q_pos[:, None] - k_pos[None, :]
    rb = rel_bias[_t5_bucket(dist)].astype(F32)
    rb = jnp.transpose(rb, (1, 2, 0)).reshape(wbuf + nr, cols)
    blk = lambda r, c: pl.BlockSpec((nb, r, c), lambda b: (b, 0, 0))
    full = lambda r, c: pl.BlockSpec((r, c), lambda b: (0, 0))
    o = pl.pallas_call(
        functools.partial(_swa_sample_kernel, nb=nb, ts=ts),
        grid=(db // nb,),
        in_specs=[full(1, cols), blk(nk, cols), blk(nr, nk), blk(nr, nk), blk(wbuf, nk), blk(wbuf, nk),
                  full(wbuf, cols), full(nr, cols)],
        out_specs=blk(cols, nk),
        out_shape=jax.ShapeDtypeStruct((db, cols, nk), F32),
        compiler_params=_params(("parallel",), 2 * nb * _nbytes((wbuf, nk), F32)),
        name="swa_sample",
    )(_col_vector(sinks, ts), _block_diag_queries(q), _pad_rows(k_new, nr), _pad_rows(v_new, nr),
      k_buf.reshape(db, wbuf, nk), v_buf.reshape(db, wbuf, nk), rb[:wbuf], rb[wbuf:])
    return _diag_heads(o, ts).astype(BF16)


def _post_norm(x, branch, gate, g, b, alpha):
    y = alpha * x + gate * branch
    mu = jnp.mean(y, axis=-1, keepdims=True)
    yc = y - mu
    var = jnp.mean(yc * yc, axis=-1, keepdims=True)
    return yc * lax.rsqrt(var + LN_EPS) * g + b


def _wo_norm_kernel(o_ref, x_ref, gate_ref, w_ref, g_ref, b_ref, y_ref, *, alpha, o_transposed):
    dims = TN_DIMS if o_transposed else (((1,), (0,)), ((), ()))
    branch = lax.dot_general(o_ref[0], w_ref[...], dims, preferred_element_type=F32)
    y_ref[0] = _post_norm(x_ref[0], branch, gate_ref[0], g_ref[...], b_ref[...], alpha)


def _wo_norm(o, x, ada, w_bf, ln_g, ln_b, alpha, tb, o_transposed=False):
    bsz, t, d = x.shape
    tm = ada.shape[1]
    nq = w_bf.shape[0]
    blk = lambda n: pl.BlockSpec((1, tb, n), lambda b, i: (b, i, 0))
    o_spec = pl.BlockSpec((1, nq, tb), lambda b, i: (b, 0, i)) if o_transposed else blk(nq)
    vec = pl.BlockSpec((1, d), lambda b, i: (0, 0))
    return pl.pallas_call(
        functools.partial(_wo_norm_kernel, alpha=alpha, o_transposed=o_transposed),
        grid=(bsz, t // tb),
        in_specs=[o_spec, blk(d), _mod_spec(tm, tb, d, 2), pl.BlockSpec(w_bf.shape, lambda b, i: (0, 0)), vec, vec],
        out_specs=blk(d),
        out_shape=jax.ShapeDtypeStruct((bsz, t, d), F32),
        compiler_params=_params(("parallel", "parallel"), _nbytes(w_bf.shape, BF16) + 5 * _nbytes((tb, d), F32)),
        name="wo_norm",
    )(o, x, ada, w_bf, ln_g.reshape(1, d), ln_b.reshape(1, d))


def _topk_ranked(s, iota_k):
    n_keys = s.shape[0]
    pos = jnp.full(s.shape, float(PEER_TOPK), F32)
    vals = []
    for a in range(PEER_TOPK):
        m = jnp.max(s, axis=0, keepdims=True)
        idx = jnp.min(jnp.where(s == m, iota_k, float(n_keys)), axis=0, keepdims=True)
        hit = iota_k == idx
        pos = jnp.where(hit, float(a), pos)
        s = jnp.where(hit, NEG_INF, s)
        vals.append(m)
    return jnp.concatenate(vals, axis=0), pos


def _topk_if_distinct(s):
    pos = jnp.full(s.shape, float(PEER_TOPK), F32)
    vals = []
    for a in range(PEER_TOPK):
        m = jnp.max(s, axis=0, keepdims=True)
        hit = s == m
        pos = jnp.where(hit, float(a), pos)
        s = jnp.where(hit, NEG_INF, s)
        vals.append(m)
    ranked = jnp.sum(jnp.where(pos < float(PEER_TOPK), 1.0, 0.0), axis=0, keepdims=True)
    return jnp.concatenate(vals, axis=0), pos, jnp.max(ranked)


def _candidate_slabs():
    k = PEER_TOPK
    slabs = [("b", 0, 0, 8), ("b", 0, 8, 8)]
    for b in range(1, 8):
        slabs.append(("b", b, 0, k // (b + 1)))
    slabs.append(("a0", None, 0, 8))
    return slabs


def _peer_route_kernel(x_ref, shift_ref, scale_ref, wq_ref, sk_ref, p2_ref, e2_ref, nt_ref, wt_ref,
                       qr_ref, s_ref, t_ref, pos_ref):
    k = PEER_TOPK
    tb = x_ref.shape[1]
    h = (x_ref[0] * (1.0 + scale_ref[0]) + shift_ref[0]).astype(BF16)
    qr_ref[...] = lax.dot_general(wq_ref[...], h, NT_DIMS, preferred_element_type=F32)
    piece = min(tb, LANES)
    iota_k = lax.broadcasted_iota(jnp.int32, (N_KEYS, piece), 0).astype(F32)
    row8 = lax.broadcasted_iota(jnp.int32, (SUBLANES, tb), 0).astype(F32)
    slabs = _candidate_slabs()
    flat = jnp.concatenate(
        [((row8 + a0) * k + b) if kind == "b" else (row8 + 8) for kind, b, a0, _ in slabs], axis=0)
    valid = jnp.concatenate([row8 < nv for _, _, _, nv in slabs], axis=0)
    n_cand = flat.shape[0]
    half = N_KEYS

    def head_body(hd, carry):
        base = pl.multiple_of(hd * 2 * half, 2 * half)
        q1 = qr_ref[pl.ds(base, half), :].astype(BF16)
        q2 = qr_ref[pl.ds(base + half, half), :].astype(BF16)
        s_ref[0] = jnp.dot(sk_ref[0], q1, preferred_element_type=F32)
        s_ref[1] = jnp.dot(sk_ref[1], q2, preferred_element_type=F32)
        pieces_idx = [(lst, slice(c0, c0 + piece)) for lst in range(2) for c0 in range(0, tb, piece)]
        ranked = None
        for lst, lanes in pieces_idx:
            t_fast, pos_fast, n_ranked = _topk_if_distinct(s_ref[lst, :, lanes])
            t_ref[lst, :, lanes] = t_fast
            pos_ref[lst, :, lanes] = pos_fast
            ranked = n_ranked if ranked is None else jnp.maximum(ranked, n_ranked)

        @pl.when(ranked > float(k))
        def _():
            for lst, lanes in pieces_idx:
                t_slow, pos_slow = _topk_ranked(s_ref[lst, :, lanes], iota_k)
                t_ref[lst, :, lanes] = t_slow
                pos_ref[lst, :, lanes] = pos_slow

        s1, s2 = s_ref[0], s_ref[1]
        t1, t2 = t_ref[0], t_ref[1]
        pos1, pos2 = pos_ref[0], pos_ref[1]
        pieces = []
        for kind, b, a0, _ in slabs:
            if kind == "b":
                pieces.append(t1[a0:a0 + 8] + t2[b:b + 1])
            else:
                pieces.append(t1[0:1] + t2[8:16])
        cand = jnp.where(valid, jnp.concatenate(pieces, axis=0), NEG_INF)
        top = t1[0:1] + t2[0:1]
        e = jnp.exp(cand - top)
        sel = jnp.zeros(cand.shape, F32)
        c = cand
        for _ in range(k):
            m = jnp.max(c, axis=0, keepdims=True)
            idx = jnp.min(jnp.where(c == m, flat, float(k * k)), axis=0, keepdims=True)
            hit = flat == idx
            sel = jnp.where(hit, 1.0, sel)
            c = jnp.where(hit, NEG_INF, c)
        z = jnp.sum(sel * e, axis=0, keepdims=True)
        n_lo = sel[0:8]
        for si in range(2, 2 + 7):
            n_lo = n_lo + sel[si * 8:(si + 1) * 8]
        last = jnp.sum(sel[n_cand - 8:], axis=0, keepdims=True)
        n_lo = n_lo + jnp.where(row8 == 0.0, last, 0.0)
        counts = jnp.concatenate([n_lo, sel[8:16]], axis=0)
        nt = jnp.zeros((N_KEYS, tb), F32)
        for a in range(k):
            nt = jnp.where(pos1 == float(a), counts[a:a + 1], nt)
        p2_ref[hd] = pos2.astype(p2_ref.dtype)
        e2_ref[hd] = jnp.exp(s2 - t2[0:1]).astype(e2_ref.dtype)
        nt_ref[hd] = nt
        wt_ref[hd] = jnp.exp(s1 - t1[0:1]) / z
        return carry

    lax.fori_loop(0, PEER_HEADS, head_body, 0)


def _peer_route(x, ada, wq_t, sub_keys_bf, tb):
    bsz, t, d = x.shape
    tm = ada.shape[1]
    n_tok = bsz * t
    nblk = t // tb
    tok = lambda b, i: b * nblk + i
    hk = pl.BlockSpec((PEER_HEADS, N_KEYS, tb), lambda b, i: (0, 0, tok(b, i)))
    return pl.pallas_call(
        _peer_route_kernel,
        grid=(bsz, nblk),
        in_specs=[pl.BlockSpec((1, tb, d), lambda b, i: (b, i, 0)), _mod_spec(tm, tb, d, 3), _mod_spec(tm, tb, d, 4),
                  pl.BlockSpec(wq_t.shape, lambda b, i: (0, 0)),
                  pl.BlockSpec(sub_keys_bf.shape, lambda b, i: (0, 0, 0))],
        out_specs=[hk, hk, hk, hk],
        out_shape=[jax.ShapeDtypeStruct((PEER_HEADS, N_KEYS, n_tok), BF16),
                   jax.ShapeDtypeStruct((PEER_HEADS, N_KEYS, n_tok), BF16),
                   jax.ShapeDtypeStruct((PEER_HEADS, N_KEYS, n_tok), F32),
                   jax.ShapeDtypeStruct((PEER_HEADS, N_KEYS, n_tok), F32)],
        scratch_shapes=[pltpu.VMEM((wq_t.shape[0], tb), F32), pltpu.VMEM((2, N_KEYS, tb), F32),
                        pltpu.VMEM((2, PEER_TOPK, tb), F32), pltpu.VMEM((2, N_KEYS, tb), F32)],
        compiler_params=_params(("parallel", "parallel"),
                                _nbytes(wq_t.shape, BF16) + 2 * _nbytes((wq_t.shape[0], tb), F32)
                                + 6 * _nbytes((PEER_HEADS, N_KEYS, tb), F32)),
        name="peer_route",
    )(x, ada, ada, wq_t, sub_keys_bf)


def _gelu_exact(x):
    return 0.5 * x * (1.0 + lax.erf(x * (2.0 ** -0.5)))


def _peer_dense_kernel(x_ref, shift_ref, scale_ref, gate_ref, g_ref, b_ref, p2_ref, e2_ref, nt_ref, wt_ref,
                       u_ref, vt_ref, y_ref, h_ref, a0_ref, a1_ref, p_ref, acc_ref, *, alpha, n_chunks, lane_chunk):
    e = pl.program_id(2)
    n_tiles = pl.num_programs(2) - 1
    tb = h_ref.shape[0]

    @pl.when(e == 0)
    def _():
        h_ref[...] = (x_ref[0] * (1.0 + scale_ref[0]) + shift_ref[0]).astype(BF16)
        acc_ref[...] = jnp.zeros_like(acc_ref)
        a1_ref[...] = jnp.zeros_like(a1_ref)

    prev_tile = jnp.maximum(e - 1, 0)

    def bcast_row(ref, hd, row, lanes):
        tile = jnp.broadcast_to(ref[hd, pl.ds(row, 1), lanes], (BF16_TILE_ROWS, lane_chunk)).astype(BF16)
        return pltpu.repeat(tile, N_KEYS // BF16_TILE_ROWS, axis=0)

    def step(a_cur, a_nxt):
        for c in range(n_chunks):
            rows = slice(c * N_KEYS, (c + 1) * N_KEYS)
            if c % DENSE_CHUNKS_PER_DOT == 0:
                mrows = slice(c * N_KEYS, (c + DENSE_CHUNKS_PER_DOT) * N_KEYS)
                a_nxt[mrows, :] = lax.dot_general(u_ref[mrows, :], h_ref[...], NT_DIMS,
                                                  preferred_element_type=F32)
            for j in range(tb // lane_chunk):
                lanes = slice(j * lane_chunk, (j + 1) * lane_chunk)
                gates = None
                for hd in range(PEER_HEADS):
                    n_sel = bcast_row(nt_ref, hd, prev_tile * n_chunks + c, lanes)
                    w = bcast_row(wt_ref, hd, prev_tile * n_chunks + c, lanes)
                    term = jnp.where(p2_ref[hd, :, lanes] < n_sel, e2_ref[hd, :, lanes], jnp.zeros((), BF16)) * w
                    gates = term if gates is None else gates + term
                p_ref[rows, lanes] = _gelu_exact(a_cur[rows, lanes]).astype(BF16) * gates
        acc_ref[...] += jnp.dot(vt_ref[...], p_ref[...], preferred_element_type=F32)

    parity = lax.rem(e, 2)

    @pl.when(parity == 0)
    def _():
        step(a1_ref, a0_ref)

    @pl.when(parity == 1)
    def _():
        step(a0_ref, a1_ref)

    @pl.when(e == n_tiles)
    def _():
        branch = acc_ref[...].T
        y_ref[0] = _post_norm(x_ref[0], branch, gate_ref[0], g_ref[...], b_ref[...], alpha)


def _peer_dense(x, ada, route, u_bf, vt_bf, ln_g, ln_b, alpha, tb, te):
    bsz, t, d = x.shape
    tm = ada.shape[1]
    nblk = t // tb
    n_exp = u_bf.shape[0]
    n_chunks = te // N_KEYS
    n_tiles = n_exp // te
    p2, e2, nt, wt = route
    tok = lambda b, i: b * nblk + i
    hk = pl.BlockSpec((PEER_HEADS, N_KEYS, tb), lambda b, i, e: (0, 0, tok(b, i)))
    xblk = pl.BlockSpec((1, tb, d), lambda b, i, e: (b, i, 0))
    vec = pl.BlockSpec((1, d), lambda b, i, e: (0, 0))
    return pl.pallas_call(
        functools.partial(_peer_dense_kernel, alpha=alpha, n_chunks=n_chunks, lane_chunk=min(tb, DENSE_LANE_CHUNK)),
        grid=(bsz, nblk, n_tiles + 1),
        in_specs=[xblk, _mod_spec(tm, tb, d, 3), _mod_spec(tm, tb, d, 4), _mod_spec(tm, tb, d, 5), vec, vec,
                  hk, hk, hk, hk,
                  pl.BlockSpec((te, d), lambda b, i, e: (jnp.minimum(e, n_tiles - 1), 0)),
                  pl.BlockSpec((d, te), lambda b, i, e: (0, jnp.maximum(e - 1, 0)))],
        out_specs=xblk,
        out_shape=jax.ShapeDtypeStruct((bsz, t, d), F32),
        scratch_shapes=[pltpu.VMEM((tb, d), BF16), pltpu.VMEM((te, tb), F32), pltpu.VMEM((te, tb), F32),
                        pltpu.VMEM((te, tb), BF16), pltpu.VMEM((d, tb), F32)],
        compiler_params=_params(("parallel", "parallel", "arbitrary"),
                                2 * _nbytes((te, d), BF16) + 2 * _nbytes((tb, d), F32)
                                + 3 * _nbytes((PEER_HEADS, N_KEYS, tb), F32) + 2 * _nbytes((te, tb), F32)),
        name="peer_dense",
    )(x, ada, ada, ada, ln_g.reshape(1, d), ln_b.reshape(1, d), p2, e2, nt, wt, u_bf, vt_bf)


def _token_block(t, want):
    tb = min(t, want)
    while t % tb:
        tb //= 2
    return tb


TOKEN_BLOCK_PROJ = 512
TOKEN_BLOCK_ROUTE = 256
TOKEN_BLOCK_DENSE = 512
EXPERT_TILE = 1024
DENSE_LANE_CHUNK = 512
DENSE_CHUNKS_PER_DOT = 8
BF16_TILE_ROWS = 2 * SUBLANES


def kernel(x_prompt, x_sample, c_prompt, c_sample, cache_sb_k, cache_sb_v, page_table, cache_swa_k, cache_swa_v,
           w_ada, b_ada, w_qkv, w_o, attn_sinks, sb_logit_bias, rel_bias, ln_mix_g, ln_mix_b, ln_ffn_g, ln_ffn_b,
           w_peer_q, peer_sub_keys, peer_u, peer_v):
    bsz, t, d = x_prompt.shape
    db, ts, _ = x_sample.shape
    depth = w_ada.shape[0]
    n_pages = page_table.shape[1]
    past_len = n_pages * cache_sb_k.shape[2]
    swa_buf = cache_swa_k.shape[2]
    keep_p = min(WINDOW, t)
    alpha = (2.0 * depth) ** 0.25
    nk = N_KV_HEADS * HEAD_DIM
    n_s = db * ts

    ada_all = _ada_params(jnp.concatenate([c_prompt, c_sample], axis=0), w_ada, b_ada)
    xp = x_prompt
    xs = x_sample.reshape(1, n_s, d)

    tb_p = _token_block(t, TOKEN_BLOCK_PROJ)
    tb_s = _token_block(n_s, TOKEN_BLOCK_PROJ)
    tr_p = _token_block(t, TOKEN_BLOCK_ROUTE)
    tr_s = _token_block(n_s, TOKEN_BLOCK_ROUTE)
    td_p = _token_block(t, TOKEN_BLOCK_DENSE)
    td_s = _token_block(n_s, TOKEN_BLOCK_DENSE)

    sb_kp, sb_vp, sb_ks, sb_vs = [], [], [], []
    swa_kp, swa_vp, swa_ks, swa_vs = [], [], [], []
    for layer in range(depth):
        ada_p = ada_all[layer, :bsz].reshape(bsz, 1, 6 * d)
        ada_s = jnp.repeat(ada_all[layer, bsz:], ts, axis=0).reshape(1, n_s, 6 * d)
        w_qkv_bf = w_qkv[layer].astype(BF16)
        w_o_bf = w_o[layer].astype(BF16)
        wq_t = w_peer_q[layer].T.astype(BF16)
        sub_keys_bf = peer_sub_keys[layer].astype(BF16)
        u_bf = peer_u[layer].astype(BF16)
        vt_bf = peer_v[layer].T.astype(BF16)

        stick_breaking = layer % 2 == 0
        qp, kp, vp, *vt_pages = _qkv_proj(xp, ada_p, w_qkv_bf, tb_p, value_pages=stick_breaking)
        qs, kn, vn = _qkv_proj(xs, ada_s, w_qkv_bf, tb_s)
        qs3 = qs.reshape(db, ts, N_HEADS * HEAD_DIM)
        kn3 = kn.reshape(db, ts, nk)
        vn3 = vn.reshape(db, ts, nk)
        slot = layer // 2
        if stick_breaking:
            op = _sb_prompt(qp, kp, vt_pages[0], sb_logit_bias[slot])
            osm = _sb_sample(qs3, kn3, vn3, cache_sb_k, cache_sb_v, slot, page_table, sb_logit_bias[slot])
            sb_kp.append(kp)
            sb_vp.append(vp)
            sb_ks.append(kn3)
            sb_vs.append(vn3)
        else:
            op = _swa_prompt(qp, kp, vp, attn_sinks[slot], rel_bias)
            osm = _swa_sample(qs3, kn3, vn3, cache_swa_k[slot], cache_swa_v[slot], attn_sinks[slot], rel_bias,
                              past_len)
            swa_kp.append(kp[:, t - keep_p:])
            swa_vp.append(vp[:, t - keep_p:])
            ck = cache_swa_k[slot].reshape(db, swa_buf, nk)
            cv = cache_swa_v[slot].reshape(db, swa_buf, nk)
            swa_ks.append(jnp.concatenate([ck, kn3], axis=1)[:, ts:ts + swa_buf])
            swa_vs.append(jnp.concatenate([cv, vn3], axis=1)[:, ts:ts + swa_buf])
        xp = _wo_norm(op, xp, ada_p, w_o_bf, ln_mix_g[layer], ln_mix_b[layer], alpha, tb_p,
                      o_transposed=stick_breaking)
        xs = _wo_norm(osm.reshape(1, n_s, -1), xs, ada_s, w_o_bf, ln_mix_g[layer], ln_mix_b[layer], alpha, tb_s)

        route_p = _peer_route(xp, ada_p, wq_t, sub_keys_bf, tr_p)
        route_s = _peer_route(xs, ada_s, wq_t, sub_keys_bf, tr_s)
        xp = _peer_dense(xp, ada_p, route_p, u_bf, vt_bf, ln_ffn_g[layer], ln_ffn_b[layer], alpha, td_p, EXPERT_TILE)
        xs = _peer_dense(xs, ada_s, route_s, u_bf, vt_bf, ln_ffn_g[layer], ln_ffn_b[layer], alpha, td_s, EXPERT_TILE)

    def heads(xs_list, lead):
        return jnp.stack(xs_list).reshape(len(xs_list), *lead, N_KV_HEADS, HEAD_DIM)

    return (xp, xs.reshape(db, ts, d),
            heads(sb_kp, (bsz, t)), heads(sb_vp, (bsz, t)), heads(sb_ks, (db, ts)), heads(sb_vs, (db, ts)),
            heads(swa_kp, (bsz, keep_p)), heads(swa_vp, (bsz, keep_p)),
            heads(swa_ks, (db, swa_buf)), heads(swa_vs, (db, swa_buf)))
```

```python
import functools
import math

import jax
import jax.numpy as jnp
from jax import lax
from jax.experimental import pallas as pl
from jax.experimental.pallas import tpu as pltpu

F32 = jnp.float32
BF16 = jnp.bfloat16
NEG_INF = float("-inf")

N_HEADS = 16
HEAD_DIM = 64
N_KV_HEADS = 4
GROUP = N_HEADS // N_KV_HEADS
Q_BLOCK = 128
WINDOW = 128
N_BUCKETS = 32
MAX_EXACT = N_BUCKETS // 2
MAX_DISTANCE = 128
PEER_HEADS = 8
PEER_TOPK = 16
N_KEYS = 128
LN_EPS = 1e-5

V7X_VMEM_BYTES = 64 * 1024 * 1024
LANES = 128
SUBLANES = 8

NT_DIMS = (((1,), (1,)), ((), ()))
TN_DIMS = (((0,), (0,)), ((), ()))


def _vmem_limit(block_bytes):
    return int(min(V7X_VMEM_BYTES * 7 // 8, 2 * block_bytes + 24 * 1024 * 1024))


def _params(semantics, block_bytes):
    return pltpu.CompilerParams(dimension_semantics=semantics, vmem_limit_bytes=_vmem_limit(block_bytes))


def _nbytes(shape, dtype):
    return math.prod(shape) * jnp.dtype(dtype).itemsize


def _ada_kernel(c_ref, w_ref, b_ref, o_ref):
    c = c_ref[...]
    s = c * jax.nn.sigmoid(c)
    o_ref[0] = jnp.dot(s, w_ref[0], precision=lax.Precision.HIGHEST, preferred_element_type=F32) + b_ref[0]


def _ada_params(c_all, w_ada, b_ada):
    depth, d, d6 = w_ada.shape
    rows = c_all.shape[0]
    tn = d6 // 6
    return pl.pallas_call(
        _ada_kernel,
        grid=(depth, d6 // tn),
        in_specs=[
            pl.BlockSpec((rows, d), lambda l, j: (0, 0)),
            pl.BlockSpec((1, d, tn), lambda l, j: (l, 0, j)),
            pl.BlockSpec((1, 1, tn), lambda l, j: (l, 0, j)),
        ],
        out_specs=pl.BlockSpec((1, rows, tn), lambda l, j: (l, 0, j)),
        out_shape=jax.ShapeDtypeStruct((depth, rows, d6), F32),
        compiler_params=_params(("parallel", "parallel"), _nbytes((d, tn), F32) + 2 * _nbytes((rows, d), F32)),
        name="ada_params",
    )(c_all, w_ada, b_ada.reshape(depth, 1, d6))


def _mod_spec(tm, tb, d, chunk):
    if tm == 1:
        return pl.BlockSpec((1, 1, d), lambda b, i, *_: (b, 0, chunk))
    return pl.BlockSpec((1, tb, d), lambda b, i, *_: (b, i, chunk))


def _qkv_kernel(x_ref, shift_ref, scale_ref, w_ref, q_ref, k_ref, v_ref, *page_refs, nq, nk):
    h = x_ref[0] * (1.0 + scale_ref[0]) + shift_ref[0]
    qkv = jnp.dot(h.astype(BF16), w_ref[...], preferred_element_type=F32)
    q = qkv[:, :nq] * (HEAD_DIM ** -0.5)
    q_ref[0] = q.astype(BF16)
    k_ref[0] = qkv[:, nq:nq + nk]
    v = qkv[:, nq + nk:]
    v_ref[0] = v
    if page_refs:
        for ref, val in zip(page_refs, (q, v)):
            val_t = val.T
            for n in range(val_t.shape[1] // Q_BLOCK):
                ref[0, n] = val_t[:, n * Q_BLOCK:(n + 1) * Q_BLOCK].astype(BF16)


def _qkv_proj(x, ada, w_bf, tb, value_pages=False):
    bsz, t, d = x.shape
    tm = ada.shape[1]
    nq = N_HEADS * HEAD_DIM
    nk = N_KV_HEADS * HEAD_DIM
    blk = lambda n: pl.BlockSpec((1, tb, n), lambda b, i: (b, i, 0))
    out_specs = [blk(nq), blk(nk), blk(nk)]
    out_shape = [jax.ShapeDtypeStruct((bsz, t, nq), BF16),
                 jax.ShapeDtypeStruct((bsz, t, nk), F32),
                 jax.ShapeDtypeStruct((bsz, t, nk), F32)]
    if value_pages:
        for n in (nq, nk):
            out_specs.append(pl.BlockSpec((1, tb // Q_BLOCK, n, Q_BLOCK), lambda b, i: (b, i, 0, 0)))
            out_shape.append(jax.ShapeDtypeStruct((bsz, t // Q_BLOCK, n, Q_BLOCK), BF16))
    return pl.pallas_call(
        functools.partial(_qkv_kernel, nq=nq, nk=nk),
        grid=(bsz, t // tb),
        in_specs=[blk(d), _mod_spec(tm, tb, d, 0), _mod_spec(tm, tb, d, 1),
                  pl.BlockSpec(w_bf.shape, lambda b, i: (0, 0))],
        out_specs=out_specs,
        out_shape=out_shape,
        compiler_params=_params(("parallel", "parallel"),
                                _nbytes(w_bf.shape, BF16) + 4 * _nbytes((tb, d), F32) + _nbytes((tb, nq + 2 * nk), F32)),
        name="qkv_proj",
    )(x, ada, ada, w_bf)


def _log_sigmoid_pair(z):
    t = jnp.log(1.0 + jnp.exp(-jnp.abs(z)))
    return jnp.minimum(z, 0.0) - t, -jnp.maximum(z, 0.0) - t


def _split_bf16(x):
    hi = x.astype(BF16)
    lo = (x - hi.astype(F32)).astype(BF16)
    return hi, lo


def _sb_prompt_kernel(bias_ref, qt_ref, k_ref, vt_ref, o_ref):
    qb = pl.program_id(1)
    blk = Q_BLOCK
    cols = GROUP * blk
    ncols = N_KV_HEADS * cols
    nk = N_KV_HEADS * HEAD_DIM
    r_i = lax.broadcasted_iota(jnp.int32, (blk, 2 * blk), 0)
    c_i = lax.broadcasted_iota(jnp.int32, (blk, 2 * blk), 1)
    later_hl = jnp.where((c_i % blk) > r_i, 1.0, 0.0).astype(BF16)
    s_idx = lax.broadcasted_iota(jnp.int32, (blk, ncols), 0)
    t_idx = lax.broadcasted_iota(jnp.int32, (blk, ncols), 1) % blk
    diag_mask = s_idx < t_idx
    q_cols = []
    for h in range(N_HEADS):
        g = h // GROUP
        pieces = []
        if g:
            pieces.append(jnp.zeros((g * HEAD_DIM, blk), BF16))
        pieces.append(qt_ref[0, 0, h * HEAD_DIM:(h + 1) * HEAD_DIM, :])
        if g + 1 < N_KV_HEADS:
            pieces.append(jnp.zeros((nk - (g + 1) * HEAD_DIM, blk), BF16))
        q_cols.append(jnp.concatenate(pieces, axis=0))
    q_all = jnp.concatenate(q_cols, axis=1)
    bias = jnp.concatenate([jnp.full((1, blk), bias_ref[h], F32) for h in range(N_HEADS)], axis=1)

    def block(j, carry, diagonal):
        o_ts, c = carry
        start = pl.multiple_of(j * blk, blk)
        kj = k_ref[0, pl.ds(start, blk), :].astype(BF16)
        z = jnp.dot(kj, q_all, preferred_element_type=F32) + bias
        log_beta, log_rest = _log_sigmoid_pair(z)
        if diagonal:
            log_rest = jnp.where(diag_mask, log_rest, 0.0)
        hi, lo = _split_bf16(log_rest)
        later = jnp.dot(later_hl, jnp.concatenate([hi, lo], axis=0), preferred_element_type=F32)
        a = jnp.exp(log_beta + later + c)
        if diagonal:
            a = jnp.where(diag_mask, a, 0.0)
        a = a.astype(BF16)
        o_ts = tuple(
            o_ts[g] + jnp.dot(vt_ref[0, j, g * HEAD_DIM:(g + 1) * HEAD_DIM, :], a[:, g * cols:(g + 1) * cols],
                              preferred_element_type=F32)
            for g in range(N_KV_HEADS))
        return o_ts, c + jnp.sum(log_rest, axis=0, keepdims=True)

    zero = (tuple(jnp.zeros((HEAD_DIM, cols), F32) for _ in range(N_KV_HEADS)), jnp.zeros((1, ncols), F32))
    carry = block(qb, zero, True)
    o_ts, _ = lax.fori_loop(0, qb, lambda i, cr: block(qb - 1 - i, cr, False), carry)
    for h in range(N_HEADS):
        g, hh = divmod(h, GROUP)
        o_ref[0, h * HEAD_DIM:(h + 1) * HEAD_DIM, :] = o_ts[g][:, hh * blk:(hh + 1) * blk].astype(o_ref.dtype)


def _sb_prompt(qt_pages, k, vt_pages, sb_bias):
    bsz, t, nk = k.shape
    nq = qt_pages.shape[2]
    return pl.pallas_call(
        _sb_prompt_kernel,
        grid=(bsz, t // Q_BLOCK),
        in_specs=[pl.BlockSpec(memory_space=pltpu.SMEM),
                  pl.BlockSpec((1, 1, nq, Q_BLOCK), lambda b, i: (b, i, 0, 0)),
                  pl.BlockSpec((1, t, nk), lambda b, i: (b, 0, 0)),
                  pl.BlockSpec((1, t // Q_BLOCK, nk, Q_BLOCK), lambda b, i: (b, 0, 0, 0))],
        out_specs=pl.BlockSpec((1, nq, Q_BLOCK), lambda b, i: (b, 0, i)),
        out_shape=jax.ShapeDtypeStruct((bsz, nq, t), BF16),
        compiler_params=_params(("parallel", "arbitrary"), 2 * _nbytes((t, nk), F32)),
        name="sb_prompt",
    )(sb_bias.astype(F32), qt_pages, k, vt_pages)


SB_PAGES_PER_STEP = 16


def _sb_sample_kernel(pt_ref, bias_ref, suffix_ref, q_ref, kn_ref, vn_ref, *refs, n_pg, ts):
    del pt_ref
    k_refs, v_refs = refs[:n_pg], refs[n_pg:2 * n_pg]
    o_ref, acc_ref, c_ref = refs[2 * n_pg:]
    i = pl.program_id(1)
    q = q_ref[0]
    bias = bias_ref[...]
    rows = q.shape[0]
    page = kn_ref.shape[2]

    def pair_update(acc, c, log_beta, log_rest, vt2, mask=None):
        hi, lo = _split_bf16(log_rest)
        sums = jnp.dot(jnp.concatenate([hi, lo], axis=1), suffix_ref[...], preferred_element_type=F32)
        a = jnp.exp(log_beta + sums[:, :2 * page] + c)
        if mask is not None:
            a = jnp.where(mask, a, 0.0)
        acc = acc + lax.dot_general(a.astype(BF16), vt2, NT_DIMS, preferred_element_type=F32)
        return acc, c + sums[:, 2 * page:]

    @pl.when(i == 0)
    def _():
        zeros = jnp.zeros((kn_ref.shape[1], page), BF16)
        kt2 = jnp.concatenate([zeros, kn_ref[0].astype(BF16)], axis=1)
        vt2 = jnp.concatenate([zeros, vn_ref[0].astype(BF16)], axis=1)
        z = jnp.dot(q, kt2, preferred_element_type=F32) + bias
        s_idx = lax.broadcasted_iota(jnp.int32, (rows, 2 * page), 1) - page
        t_idx = lax.broadcasted_iota(jnp.int32, (rows, 2 * page), 0) % ts
        mask = (s_idx >= 0) & (s_idx < t_idx)
        log_beta, log_rest = _log_sigmoid_pair(z)
        log_rest = jnp.where(mask, log_rest, 0.0)
        acc, c = pair_update(jnp.zeros(acc_ref.shape, F32), jnp.zeros(c_ref.shape, F32), log_beta, log_rest, vt2,
                             mask)
        acc_ref[...] = acc
        c_ref[...] = c

    n_pairs = n_pg // 2
    kt_all = jnp.concatenate([r[0, 0] for r in k_refs], axis=1).astype(BF16)
    vt_all = jnp.concatenate([r[0, 0] for r in v_refs], axis=1).astype(BF16)
    z = jnp.dot(q, kt_all, preferred_element_type=F32) + jnp.concatenate([bias] * n_pairs, axis=1)
    log_beta, log_rest = _log_sigmoid_pair(z)
    hi, lo = _split_bf16(log_rest)
    pair = lambda x, p: x[:, 2 * page * p:2 * page * (p + 1)]
    stacked = jnp.concatenate([jnp.concatenate([pair(hi, p), pair(lo, p)], axis=1) for p in range(n_pairs)], axis=0)
    sums = jnp.dot(stacked, suffix_ref[...], preferred_element_type=F32)
    c = c_ref[...]
    a_parts = []
    for p in range(n_pairs):
        sums_p = sums[p * rows:(p + 1) * rows]
        a_parts.append(jnp.exp(pair(log_beta, p) + sums_p[:, :2 * page] + c).astype(BF16))
        c = c + sums_p[:, 2 * page:]
    a_all = jnp.concatenate(a_parts, axis=1)
    acc = acc_ref[...] + lax.dot_general(a_all, vt_all, NT_DIMS, preferred_element_type=F32)
    acc_ref[...] = acc
    c_ref[...] = c

    @pl.when(i == pl.num_programs(1) - 1)
    def _():
        o_ref[0] = acc


def _block_diag_queries(q):
    db, ts, _ = q.shape
    qr = q.reshape(db, ts, N_KV_HEADS, GROUP, HEAD_DIM)
    qr = jnp.transpose(qr, (0, 2, 4, 3, 1)).reshape(db, N_KV_HEADS, HEAD_DIM, GROUP * ts)
    eye = jnp.eye(N_KV_HEADS, dtype=q.dtype)
    qbd = qr[:, :, :, None, :] * eye[None, :, None, :, None]
    return qbd.reshape(db, N_KV_HEADS * HEAD_DIM, N_KV_HEADS * GROUP * ts)


def _diag_heads(o, ts):
    db = o.shape[0]
    o6 = o.reshape(db, N_KV_HEADS, GROUP, ts, N_KV_HEADS, HEAD_DIM)
    od = jnp.stack([o6[:, g, :, :, g, :] for g in range(N_KV_HEADS)], axis=1)
    return jnp.transpose(od, (0, 3, 1, 2, 4)).reshape(db, ts, N_HEADS * HEAD_DIM)


def _col_vector(per_head, ts):
    return jnp.repeat(per_head.astype(F32), ts).reshape(1, N_HEADS * ts)


def _pad_rows(x, rows):
    return jnp.pad(x, ((0, 0), (0, rows - x.shape[1]), (0, 0)))


def _pair_suffix_matrix(page):
    lane = jnp.arange(2 * page)
    pos = jnp.where(lane < page, lane + page, lane - page)
    later = (pos[:, None] > pos[None, :]).astype(BF16)
    both = jnp.concatenate([later, jnp.ones((2 * page, 2 * page), BF16)], axis=1)
    return jnp.concatenate([both, both], axis=0)


def _keys_on_lanes(x, lanes):
    xt = jnp.swapaxes(x, 1, 2)
    return jnp.pad(xt, ((0, 0), (0, 0), (0, lanes - xt.shape[2])))


def _sb_sample(q, k_new, v_new, cache_k, cache_v, slot, page_table, sb_bias):
    db, ts, _ = q.shape
    n_pages = page_table.shape[1]
    n_layers, n_phys, page = cache_k.shape[:3]
    nk = N_KV_HEADS * HEAD_DIM
    n_pg = SB_PAGES_PER_STEP
    rows = N_HEADS * ts
    ckt = jnp.transpose(cache_k, (0, 1, 3, 4, 2)).reshape(n_layers, n_phys, nk, page)
    cvt = jnp.transpose(cache_v, (0, 1, 3, 4, 2)).reshape(n_layers, n_phys, nk, page)
    bias = jnp.broadcast_to(_col_vector(sb_bias, ts).reshape(rows, 1), (rows, 2 * page))

    def page_spec(p):
        return pl.BlockSpec((1, 1, nk, page),
                            lambda b, i, pt: (slot, pt[b, n_pages - 1 - (i * n_pg + p)], 0, 0))

    grid_spec = pltpu.PrefetchScalarGridSpec(
        num_scalar_prefetch=1,
        grid=(db, n_pages // n_pg),
        in_specs=[pl.BlockSpec((rows, 2 * page), lambda b, i, pt: (0, 0)),
                  pl.BlockSpec((4 * page, 4 * page), lambda b, i, pt: (0, 0)),
                  pl.BlockSpec((1, rows, nk), lambda b, i, pt: (b, 0, 0)),
                  pl.BlockSpec((1, nk, page), lambda b, i, pt: (b, 0, 0)),
                  pl.BlockSpec((1, nk, page), lambda b, i, pt: (b, 0, 0))]
                 + [page_spec(p) for p in range(n_pg)] + [page_spec(p) for p in range(n_pg)],
        out_specs=pl.BlockSpec((1, rows, nk), lambda b, i, pt: (b, 0, 0)),
        scratch_shapes=[pltpu.VMEM((rows, nk), F32), pltpu.VMEM((rows, 2 * page), F32)],
    )
    o = pl.pallas_call(
        functools.partial(_sb_sample_kernel, n_pg=n_pg, ts=ts),
        grid_spec=grid_spec,
        out_shape=jax.ShapeDtypeStruct((db, rows, nk), F32),
        compiler_params=_params(("parallel", "arbitrary"), 2 * n_pg * _nbytes((page, nk), F32)),
        name="sb_sample",
    )(page_table, bias, _pair_suffix_matrix(page), jnp.swapaxes(_block_diag_queries(q), 1, 2),
      _keys_on_lanes(k_new, page), _keys_on_lanes(v_new, page), *([ckt] * n_pg), *([cvt] * n_pg))
    return _diag_heads(o, ts).astype(BF16)


def _t5_bucket(dist):
    d = jnp.maximum(dist, 0)
    large = MAX_EXACT + (jnp.log(jnp.maximum(d, 1).astype(F32) / MAX_EXACT)
                         / math.log(MAX_DISTANCE / MAX_EXACT) * (N_BUCKETS - MAX_EXACT)).astype(jnp.int32)
    large = jnp.minimum(large, N_BUCKETS - 1)
    return jnp.where(d < MAX_EXACT, d, large)


def _swa_prompt_kernel(sink_ref, q_ref, kp_ref, kc_ref, vp_ref, vc_ref, rb_ref, o_ref):
    n = pl.program_id(1)
    w = WINDOW
    rows = GROUP * w
    i_idx = lax.broadcasted_iota(jnp.int32, (rows, 2 * w), 0) % w
    j_idx = lax.broadcasted_iota(jnp.int32, (rows, 2 * w), 1)
    dist = i_idx + w - j_idx
    mask = (dist >= 0) & (dist < w) & ((j_idx >= w) | (n > 0))
    for g in range(N_KV_HEADS):
        heads = [g * GROUP + hh for hh in range(GROUP)]
        sl = slice(g * HEAD_DIM, (g + 1) * HEAD_DIM)
        qg = jnp.concatenate([q_ref[0, :, h * HEAD_DIM:(h + 1) * HEAD_DIM] for h in heads], axis=0)
        kk = jnp.concatenate([kp_ref[0, :, sl], kc_ref[0, :, sl]], axis=0).astype(BF16)
        vv = jnp.concatenate([vp_ref[0, :, sl], vc_ref[0, :, sl]], axis=0).astype(BF16)
        rb = jnp.concatenate([rb_ref[h] for h in heads], axis=0)
        sink = jnp.concatenate([jnp.full((w, 1), sink_ref[h], F32) for h in heads], axis=0)
        z = lax.dot_general(qg, kk, NT_DIMS, preferred_element_type=F32) + rb
        z = jnp.where(mask, z, NEG_INF)
        m = jnp.maximum(jnp.max(z, axis=1, keepdims=True), sink)
        p = jnp.exp(z - m)
        denom = jnp.sum(p, axis=1, keepdims=True) + jnp.exp(sink - m)
        o = jnp.dot(p.astype(BF16), vv, preferred_element_type=F32) / denom
        for hh, h in enumerate(heads):
            o_ref[0, :, h * HEAD_DIM:(h + 1) * HEAD_DIM] = o[hh * w:(hh + 1) * w].astype(o_ref.dtype)


def _swa_prompt(q, k, v, sinks, rel_bias):
    bsz, t, nq = q.shape
    nk = k.shape[-1]
    w = WINDOW
    dist = jnp.arange(w)[:, None] + w - jnp.arange(2 * w)[None, :]
    rb = jnp.transpose(rel_bias[_t5_bucket(dist)].astype(F32), (2, 0, 1))
    cur = lambda n: pl.BlockSpec((1, w, n), lambda b, i: (b, i, 0))
    prev = lambda n: pl.BlockSpec((1, w, n), lambda b, i: (b, jnp.maximum(i - 1, 0), 0))
    return pl.pallas_call(
        _swa_prompt_kernel,
        grid=(bsz, t // w),
        in_specs=[pl.BlockSpec(memory_space=pltpu.SMEM), cur(nq), prev(nk), cur(nk), prev(nk), cur(nk),
                  pl.BlockSpec((N_HEADS, w, 2 * w), lambda b, i: (0, 0, 0))],
        out_specs=cur(nq),
        out_shape=jax.ShapeDtypeStruct((bsz, t, nq), BF16),
        compiler_params=_params(("parallel", "arbitrary"), _nbytes((N_HEADS, w, 2 * w), F32)),
        name="swa_prompt",
    )(sinks.astype(F32), q, k, k, v, v, rb)


SWA_SAMPLE_ROWS_PER_STEP = 8


def _swa_sample_kernel(sink_ref, qbd_ref, kn_ref, vn_ref, kb_ref, vb_ref, rbb_ref, rbn_ref, o_ref, *, nb, ts):
    cols = qbd_ref.shape[2]
    wbuf = kb_ref.shape[1]
    nr = kn_ref.shape[1]
    sink = sink_ref[...]
    t_b = lax.broadcasted_iota(jnp.int32, (wbuf, cols), 1) % ts
    j_b = lax.broadcasted_iota(jnp.int32, (wbuf, cols), 0)
    dist_b = t_b + wbuf - j_b
    mask_b = (dist_b >= 0) & (dist_b < WINDOW)
    t_n = lax.broadcasted_iota(jnp.int32, (nr, cols), 1) % ts
    s_n = lax.broadcasted_iota(jnp.int32, (nr, cols), 0)
    mask_n = (t_n - s_n >= 0) & (s_n < ts)
    for r in range(nb):
        qbd = qbd_ref[r]
        zb = jnp.dot(kb_ref[r].astype(BF16), qbd, preferred_element_type=F32) + rbb_ref[...]
        zn = jnp.dot(kn_ref[r].astype(BF16), qbd, preferred_element_type=F32) + rbn_ref[...]
        zb = jnp.where(mask_b, zb, NEG_INF)
        zn = jnp.where(mask_n, zn, NEG_INF)
        m = jnp.maximum(jnp.maximum(jnp.max(zb, axis=0, keepdims=True), jnp.max(zn, axis=0, keepdims=True)), sink)
        pb = jnp.exp(zb - m)
        pn = jnp.exp(zn - m)
        denom = jnp.sum(pb, axis=0, keepdims=True) + jnp.sum(pn, axis=0, keepdims=True) + jnp.exp(sink - m)
        inv = 1.0 / denom
        o = (lax.dot_general((pb * inv).astype(BF16), vb_ref[r].astype(BF16), TN_DIMS, preferred_element_type=F32)
             + lax.dot_general((pn * inv).astype(BF16), vn_ref[r].astype(BF16), TN_DIMS, preferred_element_type=F32))
        o_ref[r] = o


def _swa_sample(q, k_new, v_new, k_buf, v_buf, sinks, rel_bias, past_len):
    db, ts, _ = q.shape
    wbuf = k_buf.shape[1]
    nk = N_KV_HEADS * HEAD_DIM
    cols = N_HEADS * ts
    nr = max(SUBLANES, ts)
    nb = SWA_SAMPLE_ROWS_PER_STEP
    q_pos = past_len + jnp.arange(ts)
    k_pos = jnp.concatenate([past_len - wbuf + jnp.arange(wbuf), past_len + jnp.arange(nr)])
    dist = q_pos[:, None] - k_pos[None, :]
    rb = rel_bias[_t5_bucket(dist)].astype(F32)
    rb = jnp.transpose(rb, (1, 2, 0)).reshape(wbuf + nr, cols)
    blk = lambda r, c: pl.BlockSpec((nb, r, c), lambda b: (b, 0, 0))
    full = lambda r, c: pl.BlockSpec((r, c), lambda b: (0, 0))
    o = pl.pallas_call(
        functools.partial(_swa_sample_kernel, nb=nb, ts=ts),
        grid=(db // nb,),
        in_specs=[full(1, cols), blk(nk, cols), blk(nr, nk), blk(nr, nk), blk(wbuf, nk), blk(wbuf, nk),
                  full(wbuf, cols), full(nr, cols)],
        out_specs=blk(cols, nk),
        out_shape=jax.ShapeDtypeStruct((db, cols, nk), F32),
        compiler_params=_params(("parallel",), 2 * nb * _nbytes((wbuf, nk), F32)),
        name="swa_sample",
    )(_col_vector(sinks, ts), _block_diag_queries(q), _pad_rows(k_new, nr), _pad_rows(v_new, nr),
      k_buf.reshape(db, wbuf, nk), v_buf.reshape(db, wbuf, nk), rb[:wbuf], rb[wbuf:])
    return _diag_heads(o, ts).astype(BF16)


def _post_norm(x, branch, gate, g, b, alpha):
    y = alpha * x + gate * branch
    mu = jnp.mean(y, axis=-1, keepdims=True)
    yc = y - mu
    var = jnp.mean(yc * yc, axis=-1, keepdims=True)
    return yc * lax.rsqrt(var + LN_EPS) * g + b


def _wo_norm_kernel(o_ref, x_ref, gate_ref, w_ref, g_ref, b_ref, y_ref, *, alpha, o_transposed):
    dims = TN_DIMS if o_transposed else (((1,), (0,)), ((), ()))
    branch = lax.dot_general(o_ref[0], w_ref[...], dims, preferred_element_type=F32)
    y_ref[0] = _post_norm(x_ref[0], branch, gate_ref[0], g_ref[...], b_ref[...], alpha)


def _wo_norm(o, x, ada, w_bf, ln_g, ln_b, alpha, tb, o_transposed=False):
    bsz, t, d = x.shape
    tm = ada.shape[1]
    nq = w_bf.shape[0]
    blk = lambda n: pl.BlockSpec((1, tb, n), lambda b, i: (b, i, 0))
    o_spec = pl.BlockSpec((1, nq, tb), lambda b, i: (b, 0, i)) if o_transposed else blk(nq)
    vec = pl.BlockSpec((1, d), lambda b, i: (0, 0))
    return pl.pallas_call(
        functools.partial(_wo_norm_kernel, alpha=alpha, o_transposed=o_transposed),
        grid=(bsz, t // tb),
        in_specs=[o_spec, blk(d), _mod_spec(tm, tb, d, 2), pl.BlockSpec(w_bf.shape, lambda b, i: (0, 0)), vec, vec],
        out_specs=blk(d),
        out_shape=jax.ShapeDtypeStruct((bsz, t, d), F32),
        compiler_params=_params(("parallel", "parallel"), _nbytes(w_bf.shape, BF16) + 5 * _nbytes((tb, d), F32)),
        name="wo_norm",
    )(o, x, ada, w_bf, ln_g.reshape(1, d), ln_b.reshape(1, d))


def _topk_ranked(s, iota_k):
    n_keys = s.shape[0]
    pos = jnp.full(s.shape, float(PEER_TOPK), F32)
    vals = []
    for a in range(PEER_TOPK):
        m = jnp.max(s, axis=0, keepdims=True)
        idx = jnp.min(jnp.where(s == m, iota_k, float(n_keys)), axis=0, keepdims=True)
        hit = iota_k == idx
        pos = jnp.where(hit, float(a), pos)
        s = jnp.where(hit, NEG_INF, s)
        vals.append(m)
    return jnp.concatenate(vals, axis=0), pos


def _topk_if_distinct(s):
    pos = jnp.full(s.shape, float(PEER_TOPK), F32)
    vals = []
    for a in range(PEER_TOPK):
        m = jnp.max(s, axis=0, keepdims=True)
        hit = s == m
        pos = jnp.where(hit, float(a), pos)
        s = jnp.where(hit, NEG_INF, s)
        vals.append(m)
    ranked = jnp.sum(jnp.where(pos < float(PEER_TOPK), 1.0, 0.0), axis=0, keepdims=True)
    return jnp.concatenate(vals, axis=0), pos, jnp.max(ranked)


def _candidate_slabs():
    k = PEER_TOPK
    slabs = [("b", 0, 0, 8), ("b", 0, 8, 8)]
    for b in range(1, 8):
        slabs.append(("b", b, 0, k // (b + 1)))
    slabs.append(("a0", None, 0, 8))
    return slabs


def _peer_route_kernel(x_ref, shift_ref, scale_ref, wq_ref, sk_ref, p2_ref, e2_ref, nt_ref, wt_ref,
                       qr_ref, s_ref, t_ref, pos_ref):
    k = PEER_TOPK
    tb = x_ref.shape[1]
    h_t = (x_ref[0] * (1.0 + scale_ref[0]) + shift_ref[0]).T.astype(BF16)
    qr_ref[...] = jnp.dot(wq_ref[...], h_t, preferred_element_type=F32)
    piece = min(tb, LANES)
    iota_k = lax.broadcasted_iota(jnp.int32, (N_KEYS, piece), 0).astype(F32)
    row8 = lax.broadcasted_iota(jnp.int32, (SUBLANES, tb), 0).astype(F32)
    slabs = _candidate_slabs()
    flat = jnp.concatenate(
        [((row8 + a0) * k + b) if kind == "b" else (row8 + 8) for kind, b, a0, _ in slabs], axis=0)
    valid = jnp.concatenate([row8 < nv for _, _, _, nv in slabs], axis=0)
    n_cand = flat.shape[0]
    half = N_KEYS

    def head_body(hd, carry):
        base = pl.multiple_of(hd * 2 * half, 2 * half)
        q1 = qr_ref[pl.ds(base, half), :].astype(BF16)
        q2 = qr_ref[pl.ds(base + half, half), :].astype(BF16)
        s_ref[0] = jnp.dot(sk_ref[0], q1, preferred_element_type=F32)
        s_ref[1] = jnp.dot(sk_ref[1], q2, preferred_element_type=F32)
        pieces_idx = [(lst, slice(c0, c0 + piece)) for lst in range(2) for c0 in range(0, tb, piece)]
        ranked = None
        for lst, lanes in pieces_idx:
            t_fast, pos_fast, n_ranked = _topk_if_distinct(s_ref[lst, :, lanes])
            t_ref[lst, :, lanes] = t_fast
            pos_ref[lst, :, lanes] = pos_fast
            ranked = n_ranked if ranked is None else jnp.maximum(ranked, n_ranked)

        @pl.when(ranked > float(k))
        def _():
            for lst, lanes in pieces_idx:
                t_slow, pos_slow = _topk_ranked(s_ref[lst, :, lanes], iota_k)
                t_ref[lst, :, lanes] = t_slow
                pos_ref[lst, :, lanes] = pos_slow

        s1, s2 = s_ref[0], s_ref[1]
        t1, t2 = t_ref[0], t_ref[1]
        pos1, pos2 = pos_ref[0], pos_ref[1]
        pieces = []
        for kind, b, a0, _ in slabs:
            if kind == "b":
                pieces.append(t1[a0:a0 + 8] + t2[b:b + 1])
            else:
                pieces.append(t1[0:1] + t2[8:16])
        cand = jnp.where(valid, jnp.concatenate(pieces, axis=0), NEG_INF)
        top = t1[0:1] + t2[0:1]
        e = jnp.exp(cand - top)
        sel = jnp.zeros(cand.shape, F32)
        c = cand
        for _ in range(k):
            m = jnp.max(c, axis=0, keepdims=True)
            idx = jnp.min(jnp.where(c == m, flat, float(k * k)), axis=0, keepdims=True)
            hit = flat == idx
            sel = jnp.where(hit, 1.0, sel)
            c = jnp.where(hit, NEG_INF, c)
        z = jnp.sum(sel * e, axis=0, keepdims=True)
        n_lo = sel[0:8]
        for si in range(2, 2 + 7):
            n_lo = n_lo + sel[si * 8:(si + 1) * 8]
        last = jnp.sum(sel[n_cand - 8:], axis=0, keepdims=True)
        n_lo = n_lo + jnp.where(row8 == 0.0, last, 0.0)
        counts = jnp.concatenate([n_lo, sel[8:16]], axis=0)
        nt = jnp.zeros((N_KEYS, tb), F32)
        for a in range(k):
            nt = jnp.where(pos1 == float(a), counts[a:a + 1], nt)
        p2_ref[hd] = pos2.astype(p2_ref.dtype)
        e2_ref[hd] = jnp.exp(s2 - t2[0:1]).astype(e2_ref.dtype)
        nt_ref[hd] = nt
        wt_ref[hd] = jnp.exp(s1 - t1[0:1]) / z
        return carry

    lax.fori_loop(0, PEER_HEADS, head_body, 0)


def _peer_route(x, ada, wq_t, sub_keys_bf, tb):
    bsz, t, d = x.shape
    tm = ada.shape[1]
    n_tok = bsz * t
    nblk = t // tb
    tok = lambda b, i: b * nblk + i
    hk = pl.BlockSpec((PEER_HEADS, N_KEYS, tb), lambda b, i: (0, 0, tok(b, i)))
    return pl.pallas_call(
        _peer_route_kernel,
        grid=(bsz, nblk),
        in_specs=[pl.BlockSpec((1, tb, d), lambda b, i: (b, i, 0)), _mod_spec(tm, tb, d, 3), _mod_spec(tm, tb, d, 4),
                  pl.BlockSpec(wq_t.shape, lambda b, i: (0, 0)),
                  pl.BlockSpec(sub_keys_bf.shape, lambda b, i: (0, 0, 0))],
        out_specs=[hk, hk, hk, hk],
        out_shape=[jax.ShapeDtypeStruct((PEER_HEADS, N_KEYS, n_tok), BF16),
                   jax.ShapeDtypeStruct((PEER_HEADS, N_KEYS, n_tok), BF16),
                   jax.ShapeDtypeStruct((PEER_HEADS, N_KEYS, n_tok), F32),
                   jax.ShapeDtypeStruct((PEER_HEADS, N_KEYS, n_tok), F32)],
        scratch_shapes=[pltpu.VMEM((wq_t.shape[0], tb), F32), pltpu.VMEM((2, N_KEYS, tb), F32),
                        pltpu.VMEM((2, PEER_TOPK, tb), F32), pltpu.VMEM((2, N_KEYS, tb), F32)],
        compiler_params=_params(("parallel", "parallel"),
                                _nbytes(wq_t.shape, BF16) + 2 * _nbytes((wq_t.shape[0], tb), F32)
                                + 6 * _nbytes((PEER_HEADS, N_KEYS, tb), F32)),
        name="peer_route",
    )(x, ada, ada, wq_t, sub_keys_bf)


def _gelu_exact(x):
    return 0.5 * x * (1.0 + lax.erf(x * (2.0 ** -0.5)))


def _peer_dense_kernel(x_ref, shift_ref, scale_ref, gate_ref, g_ref, b_ref, p2_ref, e2_ref, nt_ref, wt_ref,
                       u_ref, vt_ref, y_ref, h_ref, a0_ref, a1_ref, p_ref, acc_ref, *, alpha, n_chunks, lane_chunk):
    e = pl.program_id(2)
    n_tiles = pl.num_programs(2) - 1
    tb = h_ref.shape[1]

    @pl.when(e == 0)
    def _():
        h_ref[...] = (x_ref[0] * (1.0 + scale_ref[0]) + shift_ref[0]).T.astype(BF16)
        acc_ref[...] = jnp.zeros_like(acc_ref)
        a1_ref[...] = jnp.zeros_like(a1_ref)

    prev_tile = jnp.maximum(e - 1, 0)

    def bcast_row(ref, hd, row, lanes):
        tile = jnp.broadcast_to(ref[hd, pl.ds(row, 1), lanes], (BF16_TILE_ROWS, lane_chunk)).astype(BF16)
        return pltpu.repeat(tile, N_KEYS // BF16_TILE_ROWS, axis=0)

    def step(a_cur, a_nxt):
        for c in range(n_chunks):
            rows = slice(c * N_KEYS, (c + 1) * N_KEYS)
            for j in range(tb // lane_chunk):
                lanes = slice(j * lane_chunk, (j + 1) * lane_chunk)
                gates = None
                for hd in range(PEER_HEADS):
                    n_sel = bcast_row(nt_ref, hd, prev_tile * n_chunks + c, lanes)
                    w = bcast_row(wt_ref, hd, prev_tile * n_chunks + c, lanes)
                    term = jnp.where(p2_ref[hd, :, lanes] < n_sel, e2_ref[hd, :, lanes], jnp.zeros((), BF16)) * w
                    gates = term if gates is None else gates + term
                p_ref[rows, lanes] = _gelu_exact(a_cur[rows, lanes]).astype(BF16) * gates
        a_nxt[...] = jnp.dot(u_ref[...], h_ref[...], preferred_element_type=F32)
        acc_ref[...] += jnp.dot(vt_ref[...], p_ref[...], preferred_element_type=F32)

    parity = lax.rem(e, 2)

    @pl.when(parity == 0)
    def _():
        step(a1_ref, a0_ref)

    @pl.when(parity == 1)
    def _():
        step(a0_ref, a1_ref)

    @pl.when(e == n_tiles)
    def _():
        branch = acc_ref[...].T
        y_ref[0] = _post_norm(x_ref[0], branch, gate_ref[0], g_ref[...], b_ref[...], alpha)


def _peer_dense(x, ada, route, u_bf, vt_bf, ln_g, ln_b, alpha, tb, te):
    bsz, t, d = x.shape
    tm = ada.shape[1]
    nblk = t // tb
    n_exp = u_bf.shape[0]
    n_chunks = te // N_KEYS
    n_tiles = n_exp // te
    p2, e2, nt, wt = route
    tok = lambda b, i: b * nblk + i
    hk = pl.BlockSpec((PEER_HEADS, N_KEYS, tb), lambda b, i, e: (0, 0, tok(b, i)))
    xblk = pl.BlockSpec((1, tb, d), lambda b, i, e: (b, i, 0))
    vec = pl.BlockSpec((1, d), lambda b, i, e: (0, 0))
    return pl.pallas_call(
        functools.partial(_peer_dense_kernel, alpha=alpha, n_chunks=n_chunks, lane_chunk=min(tb, DENSE_LANE_CHUNK)),
        grid=(bsz, nblk, n_tiles + 1),
        in_specs=[xblk, _mod_spec(tm, tb, d, 3), _mod_spec(tm, tb, d, 4), _mod_spec(tm, tb, d, 5), vec, vec,
                  hk, hk, hk, hk,
                  pl.BlockSpec((te, d), lambda b, i, e: (jnp.minimum(e, n_tiles - 1), 0)),
                  pl.BlockSpec((d, te), lambda b, i, e: (0, jnp.maximum(e - 1, 0)))],
        out_specs=xblk,
        out_shape=jax.ShapeDtypeStruct((bsz, t, d), F32),
        scratch_shapes=[pltpu.VMEM((d, tb), BF16), pltpu.VMEM((te, tb), F32), pltpu.VMEM((te, tb), F32),
                        pltpu.VMEM((te, tb), BF16), pltpu.VMEM((d, tb), F32)],
        compiler_params=_params(("parallel", "parallel", "arbitrary"),
                                2 * _nbytes((te, d), BF16) + 2 * _nbytes((tb, d), F32)
                                + 3 * _nbytes((PEER_HEADS, N_KEYS, tb), F32) + 2 * _nbytes((te, tb), F32)),
        name="peer_dense",
    )(x, ada, ada, ada, ln_g.reshape(1, d), ln_b.reshape(1, d), p2, e2, nt, wt, u_bf, vt_bf)


def _token_block(t, want):
    tb = min(t, want)
    while t % tb:
        tb //= 2
    return tb


TOKEN_BLOCK_PROJ = 512
TOKEN_BLOCK_ROUTE = 256
TOKEN_BLOCK_DENSE = 512
EXPERT_TILE = 1024
DENSE_LANE_CHUNK = 512
BF16_TILE_ROWS = 2 * SUBLANES


def kernel(x_prompt, x_sample, c_prompt, c_sample, cache_sb_k, cache_sb_v, page_table, cache_swa_k, cache_swa_v,
           w_ada, b_ada, w_qkv, w_o, attn_sinks, sb_logit_bias, rel_bias, ln_mix_g, ln_mix_b, ln_ffn_g, ln_ffn_b,
           w_peer_q, peer_sub_keys, peer_u, peer_v):
    bsz, t, d = x_prompt.shape
    db, ts, _ = x_sample.shape
    depth = w_ada.shape[0]
    n_pages = page_table.shape[1]
    past_len = n_pages * cache_sb_k.shape[2]
    swa_buf = cache_swa_k.shape[2]
    keep_p = min(WINDOW, t)
    alpha = (2.0 * depth) ** 0.25
    nk = N_KV_HEADS * HEAD_DIM
    n_s = db * ts

    ada_all = _ada_params(jnp.concatenate([c_prompt, c_sample], axis=0), w_ada, b_ada)
    xp = x_prompt
    xs = x_sample.reshape(1, n_s, d)

    tb_p = _token_block(t, TOKEN_BLOCK_PROJ)
    tb_s = _token_block(n_s, TOKEN_BLOCK_PROJ)
    tr_p = _token_block(t, TOKEN_BLOCK_ROUTE)
    tr_s = _token_block(n_s, TOKEN_BLOCK_ROUTE)
    td_p = _token_block(t, TOKEN_BLOCK_DENSE)
    td_s = _token_block(n_s, TOKEN_BLOCK_DENSE)

    sb_kp, sb_vp, sb_ks, sb_vs = [], [], [], []
    swa_kp, swa_vp, swa_ks, swa_vs = [], [], [], []
    for layer in range(depth):
        ada_p = ada_all[layer, :bsz].reshape(bsz, 1, 6 * d)
        ada_s = jnp.repeat(ada_all[layer, bsz:], ts, axis=0).reshape(1, n_s, 6 * d)
        w_qkv_bf = w_qkv[layer].astype(BF16)
        w_o_bf = w_o[layer].astype(BF16)
        wq_t = w_peer_q[layer].T.astype(BF16)
        sub_keys_bf = peer_sub_keys[layer].astype(BF16)
        u_bf = peer_u[layer].astype(BF16)
        vt_bf = peer_v[layer].T.astype(BF16)

        stick_breaking = layer % 2 == 0
        qp, kp, vp, *pages = _qkv_proj(xp, ada_p, w_qkv_bf, tb_p, value_pages=stick_breaking)
        qs, kn, vn = _qkv_proj(xs, ada_s, w_qkv_bf, tb_s)
        qs3 = qs.reshape(db, ts, N_HEADS * HEAD_DIM)
        kn3 = kn.reshape(db, ts, nk)
        vn3 = vn.reshape(db, ts, nk)
        slot = layer // 2
        if stick_breaking:
            op = _sb_prompt(pages[0], kp, pages[1], sb_logit_bias[slot])
            osm = _sb_sample(qs3, kn3, vn3, cache_sb_k, cache_sb_v, slot, page_table, sb_logit_bias[slot])
            sb_kp.append(kp)
            sb_vp.append(vp)
            sb_ks.append(kn3)
            sb_vs.append(vn3)
        else:
            op = _swa_prompt(qp, kp, vp, attn_sinks[slot], rel_bias)
            osm = _swa_sample(qs3, kn3, vn3, cache_swa_k[slot], cache_swa_v[slot], attn_sinks[slot], rel_bias,
                              past_len)
            swa_kp.append(kp[:, t - keep_p:])
            swa_vp.append(vp[:, t - keep_p:])
            ck = cache_swa_k[slot].reshape(db, swa_buf, nk)
            cv = cache_swa_v[slot].reshape(db, swa_buf, nk)
            swa_ks.append(jnp.concatenate([ck, kn3], axis=1)[:, ts:ts + swa_buf])
            swa_vs.append(jnp.concatenate([cv, vn3], axis=1)[:, ts:ts + swa_buf])
        xp = _wo_norm(op, xp, ada_p, w_o_bf, ln_mix_g[layer], ln_mix_b[layer], alpha, tb_p,
                      o_transposed=stick_breaking)
        xs = _wo_norm(osm.reshape(1, n_s, -1), xs, ada_s, w_o_bf, ln_mix_g[layer], ln_mix_b[layer], alpha, tb_s)

        route_p = _peer_route(xp, ada_p, wq_t, sub_keys_bf, tr_p)
        route_s = _peer_route(xs, ada_s, wq_t, sub_keys_bf, tr_s)
        xp = _peer_dense(xp, ada_p, route_p, u_bf, vt_bf, ln_ffn_g[layer], ln_ffn_b[layer], alpha, td_p, EXPERT_TILE)
        xs = _peer_dense(xs, ada_s, route_s, u_bf, vt_bf, ln_ffn_g[layer], ln_ffn_b[layer], alpha, td_s, EXPERT_TILE)

    def heads(xs_list, lead):
        return jnp.stack(xs_list).reshape(len(xs_list), *lead, N_KV_HEADS, HEAD_DIM)

    return (xp, xs.reshape(db, ts, d),
            heads(sb_kp, (bsz, t)), heads(sb_vp, (bsz, t)), heads(sb_ks, (db, ts)), heads(sb_vs, (db, ts)),
            heads(swa_kp, (bsz, keep_p)), heads(swa_vp, (bsz, keep_p)),
            heads(swa_ks, (db, swa_buf)), heads(swa_vs, (db, swa_buf)))
```

```python
import functools
import math

import jax
import jax.numpy as jnp
from jax import lax
from jax.experimental import pallas as pl
from jax.experimental.pallas import tpu as pltpu

F32 = jnp.float32
BF16 = jnp.bfloat16
NEG_INF = float("-inf")

N_HEADS = 16
HEAD_DIM = 64
N_KV_HEADS = 4
GROUP = N_HEADS // N_KV_HEADS
Q_BLOCK = 128
WINDOW = 128
N_BUCKETS = 32
MAX_EXACT = N_BUCKETS // 2
MAX_DISTANCE = 128
PEER_HEADS = 8
PEER_TOPK = 16
N_KEYS = 128
LN_EPS = 1e-5

V7X_VMEM_BYTES = 64 * 1024 * 1024
LANES = 128
SUBLANES = 8

NT_DIMS = (((1,), (1,)), ((), ()))
TN_DIMS = (((0,), (0,)), ((), ()))


def _vmem_limit(block_bytes):
    return int(min(V7X_VMEM_BYTES * 7 // 8, 2 * block_bytes + 24 * 1024 * 1024))


def _params(semantics, block_bytes):
    return pltpu.CompilerParams(dimension_semantics=semantics, vmem_limit_bytes=_vmem_limit(block_bytes))


def _nbytes(shape, dtype):
    return math.prod(shape) * jnp.dtype(dtype).itemsize


def _ada_kernel(c_ref, w_ref, b_ref, o_ref):
    c = c_ref[...]
    s = c * jax.nn.sigmoid(c)
    o_ref[0] = jnp.dot(s, w_ref[0], precision=lax.Precision.HIGHEST, preferred_element_type=F32) + b_ref[0]


def _ada_params(c_all, w_ada, b_ada):
    depth, d, d6 = w_ada.shape
    rows = c_all.shape[0]
    tn = d6 // 6
    return pl.pallas_call(
        _ada_kernel,
        grid=(depth, d6 // tn),
        in_specs=[
            pl.BlockSpec((rows, d), lambda l, j: (0, 0)),
            pl.BlockSpec((1, d, tn), lambda l, j: (l, 0, j)),
            pl.BlockSpec((1, 1, tn), lambda l, j: (l, 0, j)),
        ],
        out_specs=pl.BlockSpec((1, rows, tn), lambda l, j: (l, 0, j)),
        out_shape=jax.ShapeDtypeStruct((depth, rows, d6), F32),
        compiler_params=_params(("parallel", "parallel"), _nbytes((d, tn), F32) + 2 * _nbytes((rows, d), F32)),
        name="ada_params",
    )(c_all, w_ada, b_ada.reshape(depth, 1, d6))


def _mod_spec(tm, tb, d, chunk):
    if tm == 1:
        return pl.BlockSpec((1, 1, d), lambda b, i, *_: (b, 0, chunk))
    return pl.BlockSpec((1, tb, d), lambda b, i, *_: (b, i, chunk))


def _qkv_kernel(x_ref, shift_ref, scale_ref, w_ref, *out_refs, nq, nk, pages):
    h = x_ref[0] * (1.0 + scale_ref[0]) + shift_ref[0]
    qkv = jnp.dot(h.astype(BF16), w_ref[...], preferred_element_type=F32)
    q = qkv[:, :nq] * (HEAD_DIM ** -0.5)
    k = qkv[:, nq:nq + nk]
    v = qkv[:, nq + nk:]
    if pages:
        k_ref, v_ref, qt_ref, vt_ref = out_refs
        for ref, val in ((qt_ref, q), (vt_ref, v)):
            val_t = val.T
            for n in range(val_t.shape[1] // Q_BLOCK):
                ref[0, n] = val_t[:, n * Q_BLOCK:(n + 1) * Q_BLOCK].astype(BF16)
    else:
        q_ref, k_ref, v_ref = out_refs
        q_ref[0] = q.astype(BF16)
    k_ref[0] = k
    v_ref[0] = v


def _qkv_proj(x, ada, w_bf, tb, pages=False):
    bsz, t, d = x.shape
    tm = ada.shape[1]
    nq = N_HEADS * HEAD_DIM
    nk = N_KV_HEADS * HEAD_DIM
    blk = lambda n: pl.BlockSpec((1, tb, n), lambda b, i: (b, i, 0))
    out_specs = [blk(nk), blk(nk)]
    out_shape = [jax.ShapeDtypeStruct((bsz, t, nk), F32), jax.ShapeDtypeStruct((bsz, t, nk), F32)]
    if pages:
        for n in (nq, nk):
            out_specs.append(pl.BlockSpec((1, tb // Q_BLOCK, n, Q_BLOCK), lambda b, i: (b, i, 0, 0)))
            out_shape.append(jax.ShapeDtypeStruct((bsz, t // Q_BLOCK, n, Q_BLOCK), BF16))
    else:
        out_specs.insert(0, blk(nq))
        out_shape.insert(0, jax.ShapeDtypeStruct((bsz, t, nq), BF16))
    return pl.pallas_call(
        functools.partial(_qkv_kernel, nq=nq, nk=nk, pages=pages),
        grid=(bsz, t // tb),
        in_specs=[blk(d), _mod_spec(tm, tb, d, 0), _mod_spec(tm, tb, d, 1),
                  pl.BlockSpec(w_bf.shape, lambda b, i: (0, 0))],
        out_specs=out_specs,
        out_shape=out_shape,
        compiler_params=_params(("parallel", "parallel"),
                                _nbytes(w_bf.shape, BF16) + 4 * _nbytes((tb, d), F32) + _nbytes((tb, nq + 2 * nk), F32)),
        name="qkv_proj",
    )(x, ada, ada, w_bf)


def _log_sigmoid_pair(z):
    t = jnp.log(1.0 + jnp.exp(-jnp.abs(z)))
    return jnp.minimum(z, 0.0) - t, -jnp.maximum(z, 0.0) - t


def _split_bf16(x):
    hi = x.astype(BF16)
    lo = (x - hi.astype(F32)).astype(BF16)
    return hi, lo


def _query_columns(qt_ref):
    nk = N_KV_HEADS * HEAD_DIM
    blk = qt_ref.shape[3]
    q_cols = []
    for h in range(N_HEADS):
        g = h // GROUP
        pieces = []
        if g:
            pieces.append(jnp.zeros((g * HEAD_DIM, blk), BF16))
        pieces.append(qt_ref[0, 0, h * HEAD_DIM:(h + 1) * HEAD_DIM, :])
        if g + 1 < N_KV_HEADS:
            pieces.append(jnp.zeros((nk - (g + 1) * HEAD_DIM, blk), BF16))
        q_cols.append(jnp.concatenate(pieces, axis=0))
    return jnp.concatenate(q_cols, axis=1)


def _sb_prompt_kernel(bias_ref, qt_ref, k_ref, vt_ref, o_ref):
    qb = pl.program_id(1)
    blk = Q_BLOCK
    cols = GROUP * blk
    ncols = N_KV_HEADS * cols
    nk = N_KV_HEADS * HEAD_DIM
    r_i = lax.broadcasted_iota(jnp.int32, (blk, 2 * blk), 0)
    c_i = lax.broadcasted_iota(jnp.int32, (blk, 2 * blk), 1)
    later_hl = jnp.where((c_i % blk) > r_i, 1.0, 0.0).astype(BF16)
    s_idx = lax.broadcasted_iota(jnp.int32, (blk, ncols), 0)
    t_idx = lax.broadcasted_iota(jnp.int32, (blk, ncols), 1) % blk
    diag_mask = s_idx < t_idx
    q_all = _query_columns(qt_ref)
    bias =jnp.concatenate([jnp.full((1, blk), bias_ref[h], F32) for h in range(N_HEADS)], axis=1)

    def block(j, carry, diagonal):
        o_ts, c = carry
        start = pl.multiple_of(j * blk, blk)
        kj = k_ref[0, pl.ds(start, blk), :].astype(BF16)
        z = jnp.dot(kj, q_all, preferred_element_type=F32) + bias
        log_beta, log_rest = _log_sigmoid_pair(z)
        if diagonal:
            log_rest = jnp.where(diag_mask, log_rest, 0.0)
        hi, lo = _split_bf16(log_rest)
        later = jnp.dot(later_hl, jnp.concatenate([hi, lo], axis=0), preferred_element_type=F32)
        a = jnp.exp(log_beta + later + c)
        if diagonal:
            a = jnp.where(diag_mask, a, 0.0)
        a = a.astype(BF16)
        o_ts = tuple(
            o_ts[g] + jnp.dot(vt_ref[0, j, g * HEAD_DIM:(g + 1) * HEAD_DIM, :], a[:, g * cols:(g + 1) * cols],
                              preferred_element_type=F32)
            for g in range(N_KV_HEADS))
        return o_ts, c + jnp.sum(log_rest, axis=0, keepdims=True)

    zero = (tuple(jnp.zeros((HEAD_DIM, cols), F32) for _ in range(N_KV_HEADS)), jnp.zeros((1, ncols), F32))
    carry = block(qb, zero, True)
    o_ts, _ = lax.fori_loop(0, qb, lambda i, cr: block(qb - 1 - i, cr, False), carry)
    for h in range(N_HEADS):
        g, hh = divmod(h, GROUP)
        o_ref[0, h * HEAD_DIM:(h + 1) * HEAD_DIM, :] = o_ts[g][:, hh * blk:(hh + 1) * blk].astype(o_ref.dtype)


def _sb_prompt(qt_pages, k, vt_pages, sb_bias):
    bsz, t, nk = k.shape
    nq = qt_pages.shape[2]
    return pl.pallas_call(
        _sb_prompt_kernel,
        grid=(bsz, t // Q_BLOCK),
        in_specs=[pl.BlockSpec(memory_space=pltpu.SMEM),
                  pl.BlockSpec((1, 1, nq, Q_BLOCK), lambda b, i: (b, i, 0, 0)),
                  pl.BlockSpec((1, t, nk), lambda b, i: (b, 0, 0)),
                  pl.BlockSpec((1, t // Q_BLOCK, nk, Q_BLOCK), lambda b, i: (b, 0, 0, 0))],
        out_specs=pl.BlockSpec((1, nq, Q_BLOCK), lambda b, i: (b, 0, i)),
        out_shape=jax.ShapeDtypeStruct((bsz, nq, t), BF16),
        compiler_params=_params(("parallel", "arbitrary"), 2 * _nbytes((t, nk), F32)),
        name="sb_prompt",
    )(sb_bias.astype(F32), qt_pages, k, vt_pages)


SB_PAGES_PER_STEP = 16


def _sb_sample_kernel(pt_ref, bias_ref, suffix_ref, q_ref, kn_ref, vn_ref, *refs, n_pg, ts):
    del pt_ref
    k_refs, v_refs = refs[:n_pg], refs[n_pg:2 * n_pg]
    o_ref, acc_ref, c_ref = refs[2 * n_pg:]
    i = pl.program_id(1)
    q = q_ref[0]
    bias = bias_ref[...]
    rows = q.shape[0]
    page = kn_ref.shape[2]

    def pair_update(acc, c, log_beta, log_rest, vt2, mask=None):
        hi, lo = _split_bf16(log_rest)
        sums = jnp.dot(jnp.concatenate([hi, lo], axis=1), suffix_ref[...], preferred_element_type=F32)
        a = jnp.exp(log_beta + sums[:, :2 * page] + c)
        if mask is not None:
            a = jnp.where(mask, a, 0.0)
        acc = acc + lax.dot_general(a.astype(BF16), vt2, NT_DIMS, preferred_element_type=F32)
        return acc, c + sums[:, 2 * page:]

    @pl.when(i == 0)
    def _():
        zeros = jnp.zeros((kn_ref.shape[1], page), BF16)
        kt2 = jnp.concatenate([zeros, kn_ref[0].astype(BF16)], axis=1)
        vt2 = jnp.concatenate([zeros, vn_ref[0].astype(BF16)], axis=1)
        z = jnp.dot(q, kt2, preferred_element_type=F32) + bias
        s_idx = lax.broadcasted_iota(jnp.int32, (rows, 2 * page), 1) - page
        t_idx = lax.broadcasted_iota(jnp.int32, (rows, 2 * page), 0) % ts
        mask = (s_idx >= 0) & (s_idx < t_idx)
        log_beta, log_rest = _log_sigmoid_pair(z)
        log_rest = jnp.where(mask, log_rest, 0.0)
        acc, c = pair_update(jnp.zeros(acc_ref.shape, F32), jnp.zeros(c_ref.shape, F32), log_beta, log_rest, vt2,
                             mask)
        acc_ref[...] = acc
        c_ref[...] = c

    n_pairs = n_pg // 2
    kt_all = jnp.concatenate([r[0, 0] for r in k_refs], axis=1).astype(BF16)
    vt_all = jnp.concatenate([r[0, 0] for r in v_refs], axis=1).astype(BF16)
    z = jnp.dot(q, kt_all, preferred_element_type=F32) + jnp.concatenate([bias] * n_pairs, axis=1)
    log_beta, log_rest = _log_sigmoid_pair(z)
    hi, lo = _split_bf16(log_rest)
    pair = lambda x, p: x[:, 2 * page * p:2 * page * (p + 1)]
    stacked = jnp.concatenate([jnp.concatenate([pair(hi, p), pair(lo, p)], axis=1) for p in range(n_pairs)], axis=0)
    sums = jnp.dot(stacked, suffix_ref[...], preferred_element_type=F32)
    c = c_ref[...]
    a_parts = []
    for p in range(n_pairs):
        sums_p = sums[p * rows:(p + 1) * rows]
        a_parts.append(jnp.exp(pair(log_beta, p) + sums_p[:, :2 * page] + c).astype(BF16))
        c = c + sums_p[:, 2 * page:]
    a_all = jnp.concatenate(a_parts, axis=1)
    acc = acc_ref[...] + lax.dot_general(a_all, vt_all, NT_DIMS, preferred_element_type=F32)
    acc_ref[...] = acc
    c_ref[...] = c

    @pl.when(i == pl.num_programs(1) - 1)
    def _():
        o_ref[0] = acc


def _block_diag_queries(q):
    db, ts, _ = q.shape
    qr = q.reshape(db, ts, N_KV_HEADS, GROUP, HEAD_DIM)
    qr = jnp.transpose(qr, (0, 2, 4, 3, 1)).reshape(db, N_KV_HEADS, HEAD_DIM, GROUP * ts)
    eye = jnp.eye(N_KV_HEADS, dtype=q.dtype)
    qbd = qr[:, :, :, None, :] * eye[None, :, None, :, None]
    return qbd.reshape(db, N_KV_HEADS * HEAD_DIM, N_KV_HEADS * GROUP * ts)


def _diag_heads(o, ts):
    db = o.shape[0]
    o6 = o.reshape(db, N_KV_HEADS, GROUP, ts, N_KV_HEADS, HEAD_DIM)
    od = jnp.stack([o6[:, g, :, :, g, :] for g in range(N_KV_HEADS)], axis=1)
    return jnp.transpose(od, (0, 3, 1, 2, 4)).reshape(db, ts, N_HEADS * HEAD_DIM)


def _col_vector(per_head, ts):
    return jnp.repeat(per_head.astype(F32), ts).reshape(1, N_HEADS * ts)


def _pad_rows(x, rows):
    return jnp.pad(x, ((0, 0), (0, rows - x.shape[1]), (0, 0)))


def _pair_suffix_matrix(page):
    lane = jnp.arange(2 * page)
    pos = jnp.where(lane < page, lane + page, lane - page)
    later = (pos[:, None] > pos[None, :]).astype(BF16)
    both = jnp.concatenate([later, jnp.ones((2 * page, 2 * page), BF16)], axis=1)
    return jnp.concatenate([both, both], axis=0)


def _keys_on_lanes(x, lanes):
    xt = jnp.swapaxes(x, 1, 2)
    return jnp.pad(xt, ((0, 0), (0, 0), (0, lanes - xt.shape[2])))


def _sb_sample(q, k_new, v_new, cache_k, cache_v, slot, page_table, sb_bias):
    db, ts, _ = q.shape
    n_pages = page_table.shape[1]
    n_layers, n_phys, page = cache_k.shape[:3]
    nk = N_KV_HEADS * HEAD_DIM
    n_pg = SB_PAGES_PER_STEP
    rows = N_HEADS * ts
    ckt = jnp.transpose(cache_k, (0, 1, 3, 4, 2)).reshape(n_layers, n_phys, nk, page)
    cvt = jnp.transpose(cache_v, (0, 1, 3, 4, 2)).reshape(n_layers, n_phys, nk, page)
    bias = jnp.broadcast_to(_col_vector(sb_bias, ts).reshape(rows, 1), (rows, 2 * page))

    def page_spec(p):
        return pl.BlockSpec((1, 1, nk, page),
                            lambda b, i, pt: (slot, pt[b, n_pages - 1 - (i * n_pg + p)], 0, 0))

    grid_spec = pltpu.PrefetchScalarGridSpec(
        num_scalar_prefetch=1,
        grid=(db, n_pages // n_pg),
        in_specs=[pl.BlockSpec((rows, 2 * page), lambda b, i, pt: (0, 0)),
                  pl.BlockSpec((4 * page, 4 * page), lambda b, i, pt: (0, 0)),
                  pl.BlockSpec((1, rows, nk), lambda b, i, pt: (b, 0, 0)),
                  pl.BlockSpec((1, nk, page), lambda b, i, pt: (b, 0, 0)),
                  pl.BlockSpec((1, nk, page), lambda b, i, pt: (b, 0, 0))]
                 + [page_spec(p) for p in range(n_pg)] + [page_spec(p) for p in range(n_pg)],
        out_specs=pl.BlockSpec((1, rows, nk), lambda b, i, pt: (b, 0, 0)),
        scratch_shapes=[pltpu.VMEM((rows, nk), F32), pltpu.VMEM((rows, 2 * page), F32)],
    )
    o = pl.pallas_call(
        functools.partial(_sb_sample_kernel, n_pg=n_pg, ts=ts),
        grid_spec=grid_spec,
        out_shape=jax.ShapeDtypeStruct((db, rows, nk), F32),
        compiler_params=_params(("parallel", "arbitrary"), 2 * n_pg * _nbytes((page, nk), F32)),
        name="sb_sample",
    )(page_table, bias, _pair_suffix_matrix(page), jnp.swapaxes(_block_diag_queries(q), 1, 2),
      _keys_on_lanes(k_new, page), _keys_on_lanes(v_new, page), *([ckt] * n_pg), *([cvt] * n_pg))
    return _diag_heads(o, ts).astype(BF16)


def _t5_bucket(dist):
    d = jnp.maximum(dist, 0)
    large = MAX_EXACT + (jnp.log(jnp.maximum(d, 1).astype(F32) / MAX_EXACT)
                         / math.log(MAX_DISTANCE / MAX_EXACT) * (N_BUCKETS - MAX_EXACT)).astype(jnp.int32)
    large = jnp.minimum(large, N_BUCKETS - 1)
    return jnp.where(d < MAX_EXACT, d, large)


def _swa_prompt_kernel(sink_ref, qt_ref, kp_ref, kc_ref, vtp_ref, vtc_ref, rb_ref, o_ref):
    n = pl.program_id(1)
    w = WINDOW
    cols = GROUP * w
    ncols = N_HEADS * w
    q_all = _query_columns(qt_ref)
    kk = jnp.concatenate([kp_ref[0], kc_ref[0]], axis=0).astype(BF16)
    z = jnp.dot(kk, q_all, preferred_element_type=F32) + rb_ref[...]
    j_idx = lax.broadcasted_iota(jnp.int32, (2 * w, ncols), 0)
    i_idx = lax.broadcasted_iota(jnp.int32, (2 * w, ncols), 1) % w
    dist = i_idx + w - j_idx
    mask = (dist >= 0) & (dist < w) & ((j_idx >= w) | (n > 0))
    z = jnp.where(mask, z, NEG_INF)
    sink = jnp.concatenate([jnp.full((1, w), sink_ref[h], F32) for h in range(N_HEADS)], axis=1)
    m = jnp.maximum(jnp.max(z, axis=0, keepdims=True), sink)
    p = jnp.exp(z - m)
    inv = 1.0 / (jnp.sum(p, axis=0, keepdims=True) + jnp.exp(sink - m))
    a = (p * inv).astype(BF16)
    vt = jnp.concatenate([vtp_ref[0, 0], vtc_ref[0, 0]], axis=1)
    for g in range(N_KV_HEADS):
        o_t = jnp.dot(vt[g * HEAD_DIM:(g + 1) * HEAD_DIM, :], a[:, g * cols:(g + 1) * cols],
                      preferred_element_type=F32)
        for hh in range(GROUP):
            h = g * GROUP + hh
            o_ref[0, h * HEAD_DIM:(h + 1) * HEAD_DIM, :] = o_t[:, hh * w:(hh + 1) * w].astype(o_ref.dtype)


def _swa_prompt(qt_pages, k, vt_pages, sinks, rel_bias):
    bsz, t, nk = k.shape
    nq = qt_pages.shape[2]
    w = WINDOW
    dist = jnp.arange(w)[:, None] + w - jnp.arange(2 * w)[None, :]
    rb = rel_bias[_t5_bucket(dist)].astype(F32)
    rb = jnp.transpose(rb, (1, 2, 0)).reshape(2 * w, N_HEADS * w)
    cur = lambda b, i: (b, i, 0, 0)
    prev = lambda b, i: (b, jnp.maximum(i - 1, 0), 0, 0)
    page = lambda n, idx: pl.BlockSpec((1, 1, n, w), idx)
    rows = lambda idx: pl.BlockSpec((1, w, nk), idx)
    return pl.pallas_call(
        _swa_prompt_kernel,
        grid=(bsz, t // w),
        in_specs=[pl.BlockSpec(memory_space=pltpu.SMEM), page(nq, cur),
                  rows(lambda b, i: (b, jnp.maximum(i - 1, 0), 0)), rows(lambda b, i: (b, i, 0)),
                  page(nk, prev), page(nk, cur),
                  pl.BlockSpec((2 * w, N_HEADS * w), lambda b, i: (0, 0))],
        out_specs=pl.BlockSpec((1, nq, w), lambda b, i: (b, 0, i)),
        out_shape=jax.ShapeDtypeStruct((bsz, nq, t), BF16),
        compiler_params=_params(("parallel", "arbitrary"), 3 * _nbytes((2 * w, N_HEADS * w), F32)),
        name="swa_prompt",
    )(sinks.astype(F32), qt_pages, k, k, vt_pages, vt_pages, rb)


SWA_SAMPLE_ROWS_PER_STEP = 8


def _swa_sample_kernel(sink_ref, qbd_ref, kn_ref, vn_ref, kb_ref, vb_ref, rbb_ref, rbn_ref, o_ref, *, nb, ts):
    cols = qbd_ref.shape[2]
    wbuf = kb_ref.shape[1]
    nr = kn_ref.shape[1]
    sink = sink_ref[...]
    t_b = lax.broadcasted_iota(jnp.int32, (wbuf, cols), 1) % ts
    j_b = lax.broadcasted_iota(jnp.int32, (wbuf, cols), 0)
    dist_b = t_b + wbuf - j_b
    mask_b = (dist_b >= 0) & (dist_b < WINDOW)
    t_n = lax.broadcasted_iota(jnp.int32, (nr, cols), 1) % ts
    s_n = lax.broadcasted_iota(jnp.int32, (nr, cols), 0)
    mask_n = (t_n - s_n >= 0) & (s_n < ts)
    for r in range(nb):
        qbd = qbd_ref[r]
        zb = jnp.dot(kb_ref[r].astype(BF16), qbd, preferred_element_type=F32) + rbb_ref[...]
        zn = jnp.dot(kn_ref[r].astype(BF16), qbd, preferred_element_type=F32) + rbn_ref[...]
        zb = jnp.where(mask_b, zb, NEG_INF)
        zn = jnp.where(mask_n, zn, NEG_INF)
        m = jnp.maximum(jnp.maximum(jnp.max(zb, axis=0, keepdims=True), jnp.max(zn, axis=0, keepdims=True)), sink)
        pb = jnp.exp(zb - m)
        pn = jnp.exp(zn - m)
        denom = jnp.sum(pb, axis=0, keepdims=True) + jnp.sum(pn, axis=0, keepdims=True) + jnp.exp(sink - m)
        inv = 1.0 / denom
        o = (lax.dot_general((pb * inv).astype(BF16), vb_ref[r].astype(BF16), TN_DIMS, preferred_element_type=F32)
             + lax.dot_general((pn * inv).astype(BF16), vn_ref[r].astype(BF16), TN_DIMS, preferred_element_type=F32))
        o_ref[r] = o


def _swa_sample(q, k_new, v_new, k_buf, v_buf, sinks, rel_bias, past_len):
    db, ts, _ = q.shape
    wbuf = k_buf.shape[1]
    nk = N_KV_HEADS * HEAD_DIM
    cols = N_HEADS * ts
    nr = max(SUBLANES, ts)
    nb = SWA_SAMPLE_ROWS_PER_STEP
    q_pos = past_len + jnp.arange(ts)
    k_pos = jnp.concatenate([past_len - wbuf + jnp.arange(wbuf), past_len + jnp.arange(nr)])
    dist = q_pos[:, None] - k_pos[None, :]
    rb = rel_bias[_t5_bucket(dist)].astype(F32)
    rb = jnp.transpose(rb, (1, 2, 0)).reshape(wbuf + nr, cols)
    blk = lambda r, c: pl.BlockSpec((nb, r, c), lambda b: (b, 0, 0))
    full = lambda r, c: pl.BlockSpec((r, c), lambda b: (0, 0))
    o = pl.pallas_call(
        functools.partial(_swa_sample_kernel, nb=nb, ts=ts),
        grid=(db // nb,),
        in_specs=[full(1, cols), blk(nk, cols), blk(nr, nk), blk(nr, nk), blk(wbuf, nk), blk(wbuf, nk),
                  full(wbuf, cols), full(nr, cols)],
        out_specs=blk(cols, nk),
        out_shape=jax.ShapeDtypeStruct((db, cols, nk), F32),
        compiler_params=_params(("parallel",), 2 * nb * _nbytes((wbuf, nk), F32)),
        name="swa_sample",
    )(_col_vector(sinks, ts), _block_diag_queries(q), _pad_rows(k_new, nr), _pad_rows(v_new, nr),
      k_buf.reshape(db, wbuf, nk), v_buf.reshape(db, wbuf, nk), rb[:wbuf], rb[wbuf:])
    return _diag_heads(o, ts).astype(BF16)


def _post_norm(x, branch, gate, g, b, alpha):
    y = alpha * x + gate * branch
    mu = jnp.mean(y, axis=-1, keepdims=True)
    yc = y - mu
    var = jnp.mean(yc * yc, axis=-1, keepdims=True)
    return yc * lax.rsqrt(var + LN_EPS) * g + b


def _wo_norm_kernel(o_ref, x_ref, gate_ref, w_ref, g_ref, b_ref, y_ref, *, alpha, o_transposed):
    dims = TN_DIMS if o_transposed else (((1,), (0,)), ((), ()))
    branch = lax.dot_general(o_ref[0], w_ref[...], dims, preferred_element_type=F32)
    y_ref[0] = _post_norm(x_ref[0], branch, gate_ref[0], g_ref[...], b_ref[...], alpha)


def _wo_norm(o, x, ada, w_bf, ln_g, ln_b, alpha, tb, o_transposed=False):
    bsz, t, d = x.shape
    tm = ada.shape[1]
    nq = w_bf.shape[0]
    blk = lambda n: pl.BlockSpec((1, tb, n), lambda b, i: (b, i, 0))
    o_spec = pl.BlockSpec((1, nq, tb), lambda b, i: (b, 0, i)) if o_transposed else blk(nq)
    vec = pl.BlockSpec((1, d), lambda b, i: (0, 0))
    return pl.pallas_call(
        functools.partial(_wo_norm_kernel, alpha=alpha, o_transposed=o_transposed),
        grid=(bsz, t // tb),
        in_specs=[o_spec, blk(d), _mod_spec(tm, tb, d, 2), pl.BlockSpec(w_bf.shape, lambda b, i: (0, 0)), vec, vec],
        out_specs=blk(d),
        out_shape=jax.ShapeDtypeStruct((bsz, t, d), F32),
        compiler_params=_params(("parallel", "parallel"), _nbytes(w_bf.shape, BF16) + 5 * _nbytes((tb, d), F32)),
        name="wo_norm",
    )(o, x, ada, w_bf, ln_g.reshape(1, d), ln_b.reshape(1, d))


def _topk_ranked(s, iota_k):
    n_keys = s.shape[0]
    pos = jnp.full(s.shape, float(PEER_TOPK), F32)
    vals = []
    for a in range(PEER_TOPK):
        m = jnp.max(s, axis=0, keepdims=True)
        idx = jnp.min(jnp.where(s == m, iota_k, float(n_keys)), axis=0, keepdims=True)
        hit = iota_k == idx
        pos = jnp.where(hit, float(a), pos)
        s = jnp.where(hit, NEG_INF, s)
        vals.append(m)
    return jnp.concatenate(vals, axis=0), pos


def _topk_if_distinct(s):
    pos = jnp.full(s.shape, float(PEER_TOPK), F32)
    vals = []
    for a in range(PEER_TOPK):
        m = jnp.max(s, axis=0, keepdims=True)
        hit = s == m
        pos = jnp.where(hit, float(a), pos)
        s = jnp.where(hit, NEG_INF, s)
        vals.append(m)
    ranked = jnp.sum(jnp.where(pos < float(PEER_TOPK), 1.0, 0.0), axis=0, keepdims=True)
    return jnp.concatenate(vals, axis=0), pos, jnp.max(ranked)


def _candidate_slabs():
    k = PEER_TOPK
    slabs = [("b", 0, 0, 8), ("b", 0, 8, 8)]
    for b in range(1, 8):
        slabs.append(("b", b, 0, k // (b + 1)))
    slabs.append(("a0", None, 0, 8))
    return slabs


def _peer_route_kernel(x_ref, shift_ref, scale_ref, wq_ref, sk_ref, p2_ref, e2_ref, nt_ref, wt_ref,
                       qr_ref, s_ref, t_ref, pos_ref):
    k = PEER_TOPK
    tb = x_ref.shape[1]
    h_t = (x_ref[0] * (1.0 + scale_ref[0]) + shift_ref[0]).T.astype(BF16)
    qr_ref[...] = jnp.dot(wq_ref[...], h_t, preferred_element_type=F32)
    piece = min(tb, LANES)
    iota_k = lax.broadcasted_iota(jnp.int32, (N_KEYS, piece), 0).astype(F32)
    row8 = lax.broadcasted_iota(jnp.int32, (SUBLANES, tb), 0).astype(F32)
    slabs = _candidate_slabs()
    flat = jnp.concatenate(
        [((row8 + a0) * k + b) if kind == "b" else (row8 + 8) for kind, b, a0, _ in slabs], axis=0)
    valid = jnp.concatenate([row8 < nv for _, _, _, nv in slabs], axis=0)
    n_cand = flat.shape[0]
    half = N_KEYS

    def head_body(hd, carry):
        base = pl.multiple_of(hd * 2 * half, 2 * half)
        q1 = qr_ref[pl.ds(base, half), :].astype(BF16)
        q2 = qr_ref[pl.ds(base + half, half), :].astype(BF16)
        s_ref[0] = jnp.dot(sk_ref[0], q1, preferred_element_type=F32)
        s_ref[1] = jnp.dot(sk_ref[1], q2, preferred_element_type=F32)
        pieces_idx = [(lst, slice(c0, c0 + piece)) for lst in range(2) for c0 in range(0, tb, piece)]
        ranked = None
        for lst, lanes in pieces_idx:
            t_fast, pos_fast, n_ranked = _topk_if_distinct(s_ref[lst, :, lanes])
            t_ref[lst, :, lanes] = t_fast
            pos_ref[lst, :, lanes] = pos_fast
            ranked = n_ranked if ranked is None else jnp.maximum(ranked, n_ranked)

        @pl.when(ranked > float(k))
        def _():
            for lst, lanes in pieces_idx:
                t_slow, pos_slow = _topk_ranked(s_ref[lst, :, lanes], iota_k)
                t_ref[lst, :, lanes] = t_slow
                pos_ref[lst, :, lanes] = pos_slow

        s1, s2 = s_ref[0], s_ref[1]
        t1, t2 = t_ref[0], t_ref[1]
        pos1, pos2 = pos_ref[0], pos_ref[1]
        pieces = []
        for kind, b, a0, _ in slabs:
            if kind == "b":
                pieces.append(t1[a0:a0 + 8] + t2[b:b + 1])
            else:
                pieces.append(t1[0:1] + t2[8:16])
        cand = jnp.where(valid, jnp.concatenate(pieces, axis=0), NEG_INF)
        top = t1[0:1] + t2[0:1]
        e = jnp.exp(cand - top)
        sel = jnp.zeros(cand.shape, F32)
        c = cand
        for _ in range(k):
            m = jnp.max(c, axis=0, keepdims=True)
            idx = jnp.min(jnp.where(c == m, flat, float(k * k)), axis=0, keepdims=True)
            hit = flat == idx
            sel = jnp.where(hit, 1.0, sel)
            c = jnp.where(hit, NEG_INF, c)
        z = jnp.sum(sel * e, axis=0, keepdims=True)
        n_lo = sel[0:8]
        for si in range(2, 2 + 7):
            n_lo = n_lo + sel[si * 8:(si + 1) * 8]
        last = jnp.sum(sel[n_cand - 8:], axis=0, keepdims=True)
        n_lo = n_lo + jnp.where(row8 == 0.0, last, 0.0)
        counts = jnp.concatenate([n_lo, sel[8:16]], axis=0)
        nt = jnp.zeros((N_KEYS, tb), F32)
        for a in range(k):
            nt = jnp.where(pos1 == float(a), counts[a:a + 1], nt)
        p2_ref[hd] = pos2.astype(p2_ref.dtype)
        e2_ref[hd] = jnp.exp(s2 - t2[0:1]).astype(e2_ref.dtype)
        nt_ref[hd] = nt
        wt_ref[hd] = jnp.exp(s1 - t1[0:1]) / z
        return carry

    lax.fori_loop(0, PEER_HEADS, head_body, 0)


def _peer_route(x, ada, wq_t, sub_keys_bf, tb):
    bsz, t, d = x.shape
    tm = ada.shape[1]
    n_tok = bsz * t
    nblk = t // tb
    tok = lambda b, i: b * nblk + i
    hk = pl.BlockSpec((PEER_HEADS, N_KEYS, tb), lambda b, i: (0, 0, tok(b, i)))
    return pl.pallas_call(
        _peer_route_kernel,
        grid=(bsz, nblk),
        in_specs=[pl.BlockSpec((1, tb, d), lambda b, i: (b, i, 0)), _mod_spec(tm, tb, d, 3), _mod_spec(tm, tb, d, 4),
                  pl.BlockSpec(wq_t.shape, lambda b, i: (0, 0)),
                  pl.BlockSpec(sub_keys_bf.shape, lambda b, i: (0, 0, 0))],
        out_specs=[hk, hk, hk, hk],
        out_shape=[jax.ShapeDtypeStruct((PEER_HEADS, N_KEYS, n_tok), BF16),
                   jax.ShapeDtypeStruct((PEER_HEADS, N_KEYS, n_tok), BF16),
                   jax.ShapeDtypeStruct((PEER_HEADS, N_KEYS, n_tok), F32),
                   jax.ShapeDtypeStruct((PEER_HEADS, N_KEYS, n_tok), F32)],
        scratch_shapes=[pltpu.VMEM((wq_t.shape[0], tb), F32), pltpu.VMEM((2, N_KEYS, tb), F32),
                        pltpu.VMEM((2, PEER_TOPK, tb), F32), pltpu.VMEM((2, N_KEYS, tb), F32)],
        compiler_params=_params(("parallel", "parallel"),
                                _nbytes(wq_t.shape, BF16) + 2 * _nbytes((wq_t.shape[0], tb), F32)
                                + 6 * _nbytes((PEER_HEADS, N_KEYS, tb), F32)),
        name="peer_route",
    )(x, ada, ada, wq_t, sub_keys_bf)


def _gelu_exact(x):
    return 0.5 * x * (1.0 + lax.erf(x * (2.0 ** -0.5)))


def _peer_dense_kernel(x_ref, shift_ref, scale_ref, gate_ref, g_ref, b_ref, p2_ref, e2_ref, nt_ref, wt_ref,
                       u_ref, vt_ref, y_ref, h_ref, a0_ref, a1_ref, p_ref, acc_ref, *, alpha, n_chunks, lane_chunk):
    e = pl.program_id(2)
    n_tiles = pl.num_programs(2) - 1
    tb = h_ref.shape[1]

    @pl.when(e == 0)
    def _():
        h_ref[...] = (x_ref[0] * (1.0 + scale_ref[0]) + shift_ref[0]).T.astype(BF16)
        acc_ref[...] = jnp.zeros_like(acc_ref)
        a1_ref[...] = jnp.zeros_like(a1_ref)

    def bcast_row(ref, hd, row, lanes):
        tile = jnp.broadcast_to(ref[hd, row:row + 1, lanes], (BF16_TILE_ROWS, lane_chunk)).astype(BF16)
        return pltpu.repeat(tile, N_KEYS // BF16_TILE_ROWS, axis=0)

    def step(a_cur, a_nxt):
        for c in range(n_chunks):
            rows = slice(c * N_KEYS, (c + 1) * N_KEYS)
            for j in range(tb // lane_chunk):
                lanes = slice(j * lane_chunk, (j + 1) * lane_chunk)
                gates = None
                for hd in range(PEER_HEADS):
                    n_sel = bcast_row(nt_ref, hd, c, lanes)
                    w = bcast_row(wt_ref, hd, c, lanes)
                    term = jnp.where(p2_ref[hd, :, lanes] < n_sel, e2_ref[hd, :, lanes], jnp.zeros((), BF16)) * w
                    gates = term if gates is None else gates + term
                p_ref[rows, lanes] = _gelu_exact(a_cur[rows, lanes]).astype(BF16) * gates
        a_nxt[...] = jnp.dot(u_ref[...], h_ref[...], preferred_element_type=F32)
        acc_ref[...] += jnp.dot(vt_ref[0], p_ref[...], preferred_element_type=F32)

    parity = lax.rem(e, 2)

    @pl.when(parity == 0)
    def _():
        step(a1_ref, a0_ref)

    @pl.when(parity == 1)
    def _():
        step(a0_ref, a1_ref)

    @pl.when(e == n_tiles)
    def _():
        branch = acc_ref[...].T
        y_ref[0] = _post_norm(x_ref[0], branch, gate_ref[0], g_ref[...], b_ref[...], alpha)


def _peer_dense(x, ada, route, u_bf, vt_bf, ln_g, ln_b, alpha, tb, te):
    bsz, t, d = x.shape
    tm = ada.shape[1]
    nblk = t // tb
    n_exp = u_bf.shape[0]
    n_chunks = te // N_KEYS
    n_tiles = n_exp // te
    p2, e2, nt, wt = route
    tok = lambda b, i: b * nblk + i
    prev = lambda e: jnp.maximum(e - 1, 0)
    hk = pl.BlockSpec((PEER_HEADS, N_KEYS, tb), lambda b, i, e: (0, 0, tok(b, i)))
    hk_tile = pl.BlockSpec((PEER_HEADS, n_chunks, tb), lambda b, i, e: (0, prev(e), tok(b, i)))
    xblk = pl.BlockSpec((1, tb, d), lambda b, i, e: (b, i, 0))
    vec = pl.BlockSpec((1, d), lambda b, i, e: (0, 0))
    return pl.pallas_call(
        functools.partial(_peer_dense_kernel, alpha=alpha, n_chunks=n_chunks, lane_chunk=min(tb, DENSE_LANE_CHUNK)),
        grid=(bsz, nblk, n_tiles + 1),
        in_specs=[xblk, _mod_spec(tm, tb, d, 3), _mod_spec(tm, tb, d, 4), _mod_spec(tm, tb, d, 5), vec, vec,
                  hk, hk, hk_tile, hk_tile,
                  pl.BlockSpec((te, d), lambda b, i, e: (jnp.minimum(e, n_tiles - 1), 0)),
                  pl.BlockSpec((1, d, te), lambda b, i, e: (prev(e), 0, 0))],
        out_specs=xblk,
        out_shape=jax.ShapeDtypeStruct((bsz, t, d), F32),
        scratch_shapes=[pltpu.VMEM((d, tb), BF16), pltpu.VMEM((te, tb), F32), pltpu.VMEM((te, tb), F32),
                        pltpu.VMEM((te, tb), BF16), pltpu.VMEM((d, tb), F32)],
        compiler_params=_params(("parallel", "parallel", "arbitrary"),
                                2 * _nbytes((te, d), BF16) + 2 * _nbytes((tb, d), F32)
                                + 3 * _nbytes((PEER_HEADS, N_KEYS, tb), F32) + 2 * _nbytes((te, tb), F32)),
        name="peer_dense",
    )(x, ada, ada, ada, ln_g.reshape(1, d), ln_b.reshape(1, d), p2, e2, nt, wt, u_bf, vt_bf)


def _token_block(t, want):
    tb = min(t, want)
    while t % tb:
        tb //= 2
    return tb


TOKEN_BLOCK_PROJ = 512
TOKEN_BLOCK_ROUTE = 256
TOKEN_BLOCK_DENSE = 512
EXPERT_TILE = 1024
DENSE_LANE_CHUNK = 512
BF16_TILE_ROWS = 2 * SUBLANES


def kernel(x_prompt, x_sample, c_prompt, c_sample, cache_sb_k, cache_sb_v, page_table, cache_swa_k, cache_swa_v,
           w_ada, b_ada, w_qkv, w_o, attn_sinks, sb_logit_bias, rel_bias, ln_mix_g, ln_mix_b, ln_ffn_g, ln_ffn_b,
           w_peer_q, peer_sub_keys, peer_u, peer_v):
    bsz, t, d = x_prompt.shape
    db, ts, _ = x_sample.shape
    depth = w_ada.shape[0]
    n_pages = page_table.shape[1]
    past_len = n_pages * cache_sb_k.shape[2]
    swa_buf = cache_swa_k.shape[2]
    keep_p = min(WINDOW, t)
    alpha = (2.0 * depth) ** 0.25
    nk = N_KV_HEADS * HEAD_DIM
    n_s = db * ts

    ada_all = _ada_params(jnp.concatenate([c_prompt, c_sample], axis=0), w_ada, b_ada)
    xp = x_prompt
    xs = x_sample.reshape(1, n_s, d)

    tb_p = _token_block(t, TOKEN_BLOCK_PROJ)
    tb_s = _token_block(n_s, TOKEN_BLOCK_PROJ)
    tr_p = _token_block(t, TOKEN_BLOCK_ROUTE)
    tr_s = _token_block(n_s, TOKEN_BLOCK_ROUTE)
    td_p = _token_block(t, TOKEN_BLOCK_DENSE)
    td_s = _token_block(n_s, TOKEN_BLOCK_DENSE)

    sb_kp, sb_vp, sb_ks, sb_vs = [], [], [], []
    swa_kp, swa_vp, swa_ks, swa_vs = [], [], [], []
    for layer in range(depth):
        ada_p = ada_all[layer, :bsz].reshape(bsz, 1, 6 * d)
        ada_s = jnp.repeat(ada_all[layer, bsz:], ts, axis=0).reshape(1, n_s, 6 * d)
        w_qkv_bf = w_qkv[layer].astype(BF16)
        w_o_bf = w_o[layer].astype(BF16)
        wq_t = w_peer_q[layer].T.astype(BF16)
        sub_keys_bf = peer_sub_keys[layer].astype(BF16)
        u_bf = peer_u[layer].astype(BF16)
        vt_bf = jnp.swapaxes(peer_v[layer].reshape(-1, EXPERT_TILE, d), 1, 2).astype(BF16)

        stick_breaking = layer % 2 == 0
        kp, vp, qt_pages, vt_pages = _qkv_proj(xp, ada_p, w_qkv_bf, tb_p, pages=True)
        qs, kn, vn = _qkv_proj(xs, ada_s, w_qkv_bf, tb_s)
        qs3 = qs.reshape(db, ts, N_HEADS * HEAD_DIM)
        kn3 = kn.reshape(db, ts, nk)
        vn3 = vn.reshape(db, ts, nk)
        slot = layer // 2
        if stick_breaking:
            op = _sb_prompt(qt_pages, kp, vt_pages, sb_logit_bias[slot])
            osm = _sb_sample(qs3, kn3, vn3, cache_sb_k, cache_sb_v, slot, page_table, sb_logit_bias[slot])
            sb_kp.append(kp)
            sb_vp.append(vp)
            sb_ks.append(kn3)
            sb_vs.append(vn3)
        else:
            op = _swa_prompt(qt_pages, kp, vt_pages, attn_sinks[slot], rel_bias)
            osm = _swa_sample(qs3, kn3, vn3, cache_swa_k[slot], cache_swa_v[slot], attn_sinks[slot], rel_bias,
                              past_len)
            swa_kp.append(kp[:, t - keep_p:])
            swa_vp.append(vp[:, t - keep_p:])
            ck = cache_swa_k[slot].reshape(db, swa_buf, nk)
            cv = cache_swa_v[slot].reshape(db, swa_buf, nk)
            swa_ks.append(jnp.concatenate([ck, kn3], axis=1)[:, ts:ts + swa_buf])
            swa_vs.append(jnp.concatenate([cv, vn3], axis=1)[:, ts:ts + swa_buf])
        xp = _wo_norm(op, xp, ada_p, w_o_bf, ln_mix_g[layer], ln_mix_b[layer], alpha, tb_p, o_transposed=True)
        xs = _wo_norm(osm.reshape(1, n_s, -1), xs, ada_s, w_o_bf, ln_mix_g[layer], ln_mix_b[layer], alpha, tb_s)

        route_p = _peer_route(xp, ada_p, wq_t, sub_keys_bf, tr_p)
        route_s = _peer_route(xs, ada_s, wq_t, sub_keys_bf, tr_s)
        xp = _peer_dense(xp, ada_p, route_p, u_bf, vt_bf, ln_ffn_g[layer], ln_ffn_b[layer], alpha, td_p, EXPERT_TILE)
        xs = _peer_dense(xs, ada_s, route_s, u_bf, vt_bf, ln_ffn_g[layer], ln_ffn_b[layer], alpha, td_s, EXPERT_TILE)

    def heads(xs_list, lead):
        return jnp.stack(xs_list).reshape(len(xs_list), *lead, N_KV_HEADS, HEAD_DIM)

    return (xp, xs.reshape(db, ts, d),
            heads(sb_kp, (bsz, t)), heads(sb_vp, (bsz, t)), heads(sb_ks, (db, ts)), heads(sb_vs, (db, ts)),
            heads(swa_kp, (bsz, keep_p)), heads(swa_vp, (bsz, keep_p)),
            heads(swa_ks, (db, swa_buf)), heads(swa_vs, (db, swa_buf)))
```

```python
import functools
import math

import jax
import jax.numpy as jnp
from jax import lax
from jax.experimental import pallas as pl
from jax.experimental.pallas import tpu as pltpu

F32 = jnp.float32
BF16 = jnp.bfloat16
NEG_INF = float("-inf")

N_HEADS = 16
HEAD_DIM = 64
N_KV_HEADS = 4
GROUP = N_HEADS // N_KV_HEADS
Q_BLOCK = 128
WINDOW = 128
N_BUCKETS = 32
MAX_EXACT = N_BUCKETS // 2
MAX_DISTANCE = 128
PEER_HEADS = 8
PEER_TOPK = 16
N_KEYS = 128
LN_EPS = 1e-5

V7X_VMEM_BYTES = 64 * 1024 * 1024
LANES = 128
SUBLANES = 8

NT_DIMS = (((1,), (1,)), ((), ()))
TN_DIMS = (((0,), (0,)), ((), ()))


def _vmem_limit(block_bytes):
    return int(min(V7X_VMEM_BYTES * 7 // 8, 2 * block_bytes + 24 * 1024 * 1024))


def _params(semantics, block_bytes):
    return pltpu.CompilerParams(dimension_semantics=semantics, vmem_limit_bytes=_vmem_limit(block_bytes))


def _nbytes(shape, dtype):
    return math.prod(shape) * jnp.dtype(dtype).itemsize


def _ada_kernel(c_ref, w_ref, b_ref, o_ref):
    c = c_ref[...]
    s = c * jax.nn.sigmoid(c)
    o_ref[0] = jnp.dot(s, w_ref[0], precision=lax.Precision.HIGHEST, preferred_element_type=F32) + b_ref[0]


def _ada_params(c_all, w_ada, b_ada):
    depth, d, d6 = w_ada.shape
    rows = c_all.shape[0]
    tn = d6 // 6
    return pl.pallas_call(
        _ada_kernel,
        grid=(depth, d6 // tn),
        in_specs=[
            pl.BlockSpec((rows, d), lambda l, j: (0, 0)),
            pl.BlockSpec((1, d, tn), lambda l, j: (l, 0, j)),
            pl.BlockSpec((1, 1, tn), lambda l, j: (l, 0, j)),
        ],
        out_specs=pl.BlockSpec((1, rows, tn), lambda l, j: (l, 0, j)),
        out_shape=jax.ShapeDtypeStruct((depth, rows, d6), F32),
        compiler_params=_params(("parallel", "parallel"), _nbytes((d, tn), F32) + 2 * _nbytes((rows, d), F32)),
        name="ada_params",
    )(c_all, w_ada, b_ada.reshape(depth, 1, d6))


def _mod_spec(tm, tb, d, chunk):
    if tm == 1:
        return pl.BlockSpec((1, 1, d), lambda b, i, *_: (b, 0, chunk))
    return pl.BlockSpec((1, tb, d), lambda b, i, *_: (b, i, chunk))


def _qkv_kernel(x_ref, shift_ref, scale_ref, w_ref, *out_refs, nq, nk, pages):
    h = x_ref[0] * (1.0 + scale_ref[0]) + shift_ref[0]
    qkv = jnp.dot(h.astype(BF16), w_ref[...], preferred_element_type=F32)
    q = qkv[:, :nq] * (HEAD_DIM ** -0.5)
    k = qkv[:, nq:nq + nk]
    v = qkv[:, nq + nk:]
    if pages:
        k_ref, v_ref, qt_ref, vt_ref = out_refs
        for ref, val in ((qt_ref, q), (vt_ref, v)):
            val_t = val.T
            for n in range(val_t.shape[1] // Q_BLOCK):
                ref[0, n] = val_t[:, n * Q_BLOCK:(n + 1) * Q_BLOCK].astype(BF16)
    else:
        q_ref, k_ref, v_ref = out_refs
        q_ref[0] = q.astype(BF16)
    k_ref[0] = k
    v_ref[0] = v


def _qkv_proj(x, ada, w_bf, tb, pages=False):
    bsz, t, d = x.shape
    tm = ada.shape[1]
    nq = N_HEADS * HEAD_DIM
    nk = N_KV_HEADS * HEAD_DIM
    blk = lambda n: pl.BlockSpec((1, tb, n), lambda b, i: (b, i, 0))
    out_specs = [blk(nk), blk(nk)]
    out_shape = [jax.ShapeDtypeStruct((bsz, t, nk), F32), jax.ShapeDtypeStruct((bsz, t, nk), F32)]
    if pages:
        for n in (nq, nk):
            out_specs.append(pl.BlockSpec((1, tb // Q_BLOCK, n, Q_BLOCK), lambda b, i: (b, i, 0, 0)))
            out_shape.append(jax.ShapeDtypeStruct((bsz, t // Q_BLOCK, n, Q_BLOCK), BF16))
    else:
        out_specs.insert(0, blk(nq))
        out_shape.insert(0, jax.ShapeDtypeStruct((bsz, t, nq), BF16))
    return pl.pallas_call(
        functools.partial(_qkv_kernel, nq=nq, nk=nk, pages=pages),
        grid=(bsz, t // tb),
        in_specs=[blk(d), _mod_spec(tm, tb, d, 0), _mod_spec(tm, tb, d, 1),
                  pl.BlockSpec(w_bf.shape, lambda b, i: (0, 0))],
        out_specs=out_specs,
        out_shape=out_shape,
        compiler_params=_params(("parallel", "parallel"),
                                _nbytes(w_bf.shape, BF16) + 4 * _nbytes((tb, d), F32) + _nbytes((tb, nq + 2 * nk), F32)),
        name="qkv_proj",
    )(x, ada, ada, w_bf)


def _log_sigmoid_neg(z):
    return -jnp.maximum(z, 0.0) - jnp.log(1.0 + jnp.exp(-jnp.abs(z)))


def _split_bf16(x):
    hi = x.astype(BF16)
    lo = (x - hi.astype(F32)).astype(BF16)
    return hi, lo


def _query_columns(qt_ref):
    nk = N_KV_HEADS * HEAD_DIM
    blk = qt_ref.shape[3]
    q_cols = []
    for h in range(N_HEADS):
        g = h // GROUP
        pieces = []
        if g:
            pieces.append(jnp.zeros((g * HEAD_DIM, blk), BF16))
        pieces.append(qt_ref[0, 0, h * HEAD_DIM:(h + 1) * HEAD_DIM, :])
        if g + 1 < N_KV_HEADS:
            pieces.append(jnp.zeros((nk - (g + 1) * HEAD_DIM, blk), BF16))
        q_cols.append(jnp.concatenate(pieces, axis=0))
    return jnp.concatenate(q_cols, axis=1)


def _sb_prompt_kernel(bias_ref, qt_ref, k_ref, vt_ref, o_ref):
    qb = pl.program_id(1)
    blk = Q_BLOCK
    cols = GROUP * blk
    ncols = N_KV_HEADS * cols
    nk = N_KV_HEADS * HEAD_DIM
    r_i = lax.broadcasted_iota(jnp.int32, (blk, 2 * blk), 0)
    c_i = lax.broadcasted_iota(jnp.int32, (blk, 2 * blk), 1)
    suffix_hl = jnp.where((c_i % blk) >= r_i, 1.0, 0.0).astype(BF16)
    s_idx = lax.broadcasted_iota(jnp.int32, (blk, ncols), 0)
    t_idx = lax.broadcasted_iota(jnp.int32, (blk, ncols), 1) % blk
    diag_mask = s_idx < t_idx
    q_all = _query_columns(qt_ref)
    bias =jnp.concatenate([jnp.full((1, blk), bias_ref[h], F32) for h in range(N_HEADS)], axis=1)

    def block(j, carry, diagonal):
        o_ts, c = carry
        start = pl.multiple_of(j * blk, blk)
        kj = k_ref[0, pl.ds(start, blk), :].astype(BF16)
        z = jnp.dot(kj, q_all, preferred_element_type=F32) + bias
        log_rest = _log_sigmoid_neg(z)
        if diagonal:
            log_rest = jnp.where(diag_mask, log_rest, 0.0)
        hi, lo = _split_bf16(log_rest)
        suffix = jnp.dot(suffix_hl, jnp.concatenate([hi, lo], axis=0), preferred_element_type=F32)
        a = jnp.exp(z + suffix + c)
        if diagonal:
            a = jnp.where(diag_mask, a, 0.0)
        a = a.astype(BF16)
        o_ts = tuple(
            o_ts[g] + jnp.dot(vt_ref[0, j, g * HEAD_DIM:(g + 1) * HEAD_DIM, :], a[:, g * cols:(g + 1) * cols],
                              preferred_element_type=F32)
            for g in range(N_KV_HEADS))
        return o_ts, c + jnp.sum(log_rest, axis=0, keepdims=True)

    zero = (tuple(jnp.zeros((HEAD_DIM, cols), F32) for _ in range(N_KV_HEADS)), jnp.zeros((1, ncols), F32))
    carry = block(qb, zero, True)
    o_ts, _ = lax.fori_loop(0, qb, lambda i, cr: block(qb - 1 - i, cr, False), carry)
    for h in range(N_HEADS):
        g, hh = divmod(h, GROUP)
        o_ref[0, h * HEAD_DIM:(h + 1) * HEAD_DIM, :] = o_ts[g][:, hh * blk:(hh + 1) * blk].astype(o_ref.dtype)


def _sb_prompt(qt_pages, k, vt_pages, sb_bias):
    bsz, t, nk = k.shape
    nq = qt_pages.shape[2]
    return pl.pallas_call(
        _sb_prompt_kernel,
        grid=(bsz, t // Q_BLOCK),
        in_specs=[pl.BlockSpec(memory_space=pltpu.SMEM),
                  pl.BlockSpec((1, 1, nq, Q_BLOCK), lambda b, i: (b, i, 0, 0)),
                  pl.BlockSpec((1, t, nk), lambda b, i: (b, 0, 0)),
                  pl.BlockSpec((1, t // Q_BLOCK, nk, Q_BLOCK), lambda b, i: (b, 0, 0, 0))],
        out_specs=pl.BlockSpec((1, nq, Q_BLOCK), lambda b, i: (b, 0, i)),
        out_shape=jax.ShapeDtypeStruct((bsz, nq, t), BF16),
        compiler_params=_params(("parallel", "arbitrary"), 2 * _nbytes((t, nk), F32)),
        name="sb_prompt",
    )(sb_bias.astype(F32), qt_pages, k, vt_pages)


SB_PAGES_PER_STEP = 16


def _sb_sample_kernel(pt_ref, bias_ref, suffix_ref, q_ref, kn_ref, vn_ref, *refs, n_pg, ts):
    del pt_ref
    k_refs, v_refs = refs[:n_pg], refs[n_pg:2 * n_pg]
    o_ref, acc_ref, c_ref = refs[2 * n_pg:]
    i = pl.program_id(1)
    q = q_ref[0]
    bias = bias_ref[...]
    rows = q.shape[0]
    page = kn_ref.shape[2]

    def pair_update(acc, c, z, log_rest, vt2, mask=None):
        hi, lo = _split_bf16(log_rest)
        sums = jnp.dot(jnp.concatenate([hi, lo], axis=1), suffix_ref[...], preferred_element_type=F32)
        a = jnp.exp(z + sums[:, :2 * page] + c)
        if mask is not None:
            a = jnp.where(mask, a, 0.0)
        acc = acc + lax.dot_general(a.astype(BF16), vt2, NT_DIMS, preferred_element_type=F32)
        return acc, c + sums[:, 2 * page:]

    @pl.when(i == 0)
    def _():
        zeros = jnp.zeros((kn_ref.shape[1], page), BF16)
        kt2 = jnp.concatenate([zeros, kn_ref[0].astype(BF16)], axis=1)
        vt2 = jnp.concatenate([zeros, vn_ref[0].astype(BF16)], axis=1)
        z = jnp.dot(q, kt2, preferred_element_type=F32) + bias
        s_idx = lax.broadcasted_iota(jnp.int32, (rows, 2 * page), 1) - page
        t_idx = lax.broadcasted_iota(jnp.int32, (rows, 2 * page), 0) % ts
        mask = (s_idx >= 0) & (s_idx < t_idx)
        log_rest = jnp.where(mask, _log_sigmoid_neg(z), 0.0)
        acc, c = pair_update(jnp.zeros(acc_ref.shape, F32), jnp.zeros(c_ref.shape, F32), z, log_rest, vt2, mask)
        acc_ref[...] = acc
        c_ref[...] = c

    n_pairs = n_pg // 2
    kt_all = jnp.concatenate([r[0, 0] for r in k_refs], axis=1).astype(BF16)
    vt_all = jnp.concatenate([r[0, 0] for r in v_refs], axis=1).astype(BF16)
    z = jnp.dot(q, kt_all, preferred_element_type=F32) + jnp.concatenate([bias] * n_pairs, axis=1)
    hi, lo = _split_bf16(_log_sigmoid_neg(z))
    pair = lambda x, p: x[:, 2 * page * p:2 * page * (p + 1)]
    stacked = jnp.concatenate([jnp.concatenate([pair(hi, p), pair(lo, p)], axis=1) for p in range(n_pairs)], axis=0)
    sums = jnp.dot(stacked, suffix_ref[...], preferred_element_type=F32)
    c = c_ref[...]
    a_parts = []
    for p in range(n_pairs):
        sums_p = sums[p * rows:(p + 1) * rows]
        a_parts.append(jnp.exp(pair(z, p) + sums_p[:, :2 * page] + c).astype(BF16))
        c = c + sums_p[:, 2 * page:]
    a_all = jnp.concatenate(a_parts, axis=1)
    acc = acc_ref[...] + lax.dot_general(a_all, vt_all, NT_DIMS, preferred_element_type=F32)
    acc_ref[...] = acc
    c_ref[...] = c

    @pl.when(i == pl.num_programs(1) - 1)
    def _():
        o_ref[0] = acc


def _block_diag_queries(q):
    db, ts, _ = q.shape
    qr = q.reshape(db, ts, N_KV_HEADS, GROUP, HEAD_DIM)
    qr = jnp.transpose(qr, (0, 2, 4, 3, 1)).reshape(db, N_KV_HEADS, HEAD_DIM, GROUP * ts)
    eye = jnp.eye(N_KV_HEADS, dtype=q.dtype)
    qbd = qr[:, :, :, None, :] * eye[None, :, None, :, None]
    return qbd.reshape(db, N_KV_HEADS * HEAD_DIM, N_KV_HEADS * GROUP * ts)


def _diag_heads(o, ts):
    db = o.shape[0]
    o6 = o.reshape(db, N_KV_HEADS, GROUP, ts, N_KV_HEADS, HEAD_DIM)
    od = jnp.stack([o6[:, g, :, :, g, :] for g in range(N_KV_HEADS)], axis=1)
    return jnp.transpose(od, (0, 3, 1, 2, 4)).reshape(db, ts, N_HEADS * HEAD_DIM)


def _col_vector(per_head, ts):
    return jnp.repeat(per_head.astype(F32), ts).reshape(1, N_HEADS * ts)


def _pad_rows(x, rows):
    return jnp.pad(x, ((0, 0), (0, rows - x.shape[1]), (0, 0)))


def _pair_suffix_matrix(page):
    lane = jnp.arange(2 * page)
    pos = jnp.where(lane < page, lane + page, lane - page)
    suffix = (pos[:, None] >= pos[None, :]).astype(BF16)
    both = jnp.concatenate([suffix, jnp.ones((2 * page, 2 * page), BF16)], axis=1)
    return jnp.concatenate([both, both], axis=0)


def _keys_on_lanes(x, lanes):
    xt = jnp.swapaxes(x, 1, 2)
    return jnp.pad(xt, ((0, 0), (0, 0), (0, lanes - xt.shape[2])))


def _sb_sample(q, k_new, v_new, cache_k, cache_v, slot, page_table, sb_bias):
    db, ts, _ = q.shape
    n_pages = page_table.shape[1]
    n_layers, n_phys, page = cache_k.shape[:3]
    nk = N_KV_HEADS * HEAD_DIM
    n_pg = SB_PAGES_PER_STEP
    rows = N_HEADS * ts
    ckt = jnp.transpose(cache_k, (0, 1, 3, 4, 2)).reshape(n_layers, n_phys, nk, page)
    cvt = jnp.transpose(cache_v, (0, 1, 3, 4, 2)).reshape(n_layers, n_phys, nk, page)
    bias = jnp.broadcast_to(_col_vector(sb_bias, ts).reshape(rows, 1), (rows, 2 * page))

    def page_spec(p):
        return pl.BlockSpec((1, 1, nk, page),
                            lambda b, i, pt: (slot, pt[b, n_pages - 1 - (i * n_pg + p)], 0, 0))

    grid_spec = pltpu.PrefetchScalarGridSpec(
        num_scalar_prefetch=1,
        grid=(db, n_pages // n_pg),
        in_specs=[pl.BlockSpec((rows, 2 * page), lambda b, i, pt: (0, 0)),
                  pl.BlockSpec((4 * page, 4 * page), lambda b, i, pt: (0, 0)),
                  pl.BlockSpec((1, rows, nk), lambda b, i, pt: (b, 0, 0)),
                  pl.BlockSpec((1, nk, page), lambda b, i, pt: (b, 0, 0)),
                  pl.BlockSpec((1, nk, page), lambda b, i, pt: (b, 0, 0))]
                 + [page_spec(p) for p in range(n_pg)] + [page_spec(p) for p in range(n_pg)],
        out_specs=pl.BlockSpec((1, rows, nk), lambda b, i, pt: (b, 0, 0)),
        scratch_shapes=[pltpu.VMEM((rows, nk), F32), pltpu.VMEM((rows, 2 * page), F32)],
    )
    o = pl.pallas_call(
        functools.partial(_sb_sample_kernel, n_pg=n_pg, ts=ts),
        grid_spec=grid_spec,
        out_shape=jax.ShapeDtypeStruct((db, rows, nk), F32),
        compiler_params=_params(("parallel", "arbitrary"), 2 * n_pg * _nbytes((page, nk), F32)),
        name="sb_sample",
    )(page_table, bias, _pair_suffix_matrix(page), jnp.swapaxes(_block_diag_queries(q), 1, 2),
      _keys_on_lanes(k_new, page), _keys_on_lanes(v_new, page), *([ckt] * n_pg), *([cvt] * n_pg))
    return _diag_heads(o, ts).astype(BF16)


def _t5_bucket(dist):
    d = jnp.maximum(dist, 0)
    large = MAX_EXACT + (jnp.log(jnp.maximum(d, 1).astype(F32) / MAX_EXACT)
                         / math.log(MAX_DISTANCE / MAX_EXACT) * (N_BUCKETS - MAX_EXACT)).astype(jnp.int32)
    large = jnp.minimum(large, N_BUCKETS - 1)
    return jnp.where(d < MAX_EXACT, d, large)


def _swa_prompt_kernel(sink_ref, qt_ref, kp_ref, kc_ref, vtp_ref, vtc_ref, rb_ref, o_ref):
    n = pl.program_id(1)
    w = WINDOW
    cols = GROUP * w
    ncols = N_HEADS * w
    q_all = _query_columns(qt_ref)
    kk = jnp.concatenate([kp_ref[0], kc_ref[0]], axis=0).astype(BF16)
    z = jnp.dot(kk, q_all, preferred_element_type=F32) + rb_ref[...]
    j_idx = lax.broadcasted_iota(jnp.int32, (2 * w, ncols), 0)
    i_idx = lax.broadcasted_iota(jnp.int32, (2 * w, ncols), 1) % w
    dist = i_idx + w - j_idx
    mask = (dist >= 0) & (dist < w) & ((j_idx >= w) | (n > 0))
    z = jnp.where(mask, z, NEG_INF)
    sink = jnp.concatenate([jnp.full((1, w), sink_ref[h], F32) for h in range(N_HEADS)], axis=1)
    m = jnp.maximum(jnp.max(z, axis=0, keepdims=True), sink)
    p = jnp.exp(z - m)
    inv = 1.0 / (jnp.sum(p, axis=0, keepdims=True) + jnp.exp(sink - m))
    a = (p * inv).astype(BF16)
    vt = jnp.concatenate([vtp_ref[0, 0], vtc_ref[0, 0]], axis=1)
    for g in range(N_KV_HEADS):
        o_t = jnp.dot(vt[g * HEAD_DIM:(g + 1) * HEAD_DIM, :], a[:, g * cols:(g + 1) * cols],
                      preferred_element_type=F32)
        for hh in range(GROUP):
            h = g * GROUP + hh
            o_ref[0, h * HEAD_DIM:(h + 1) * HEAD_DIM, :] = o_t[:, hh * w:(hh + 1) * w].astype(o_ref.dtype)


def _swa_prompt(qt_pages, k, vt_pages, sinks, rel_bias):
    bsz, t, nk = k.shape
    nq = qt_pages.shape[2]
    w = WINDOW
    dist = jnp.arange(w)[:, None] + w - jnp.arange(2 * w)[None, :]
    rb = rel_bias[_t5_bucket(dist)].astype(F32)
    rb = jnp.transpose(rb, (1, 2, 0)).reshape(2 * w, N_HEADS * w)
    cur = lambda b, i: (b, i, 0, 0)
    prev = lambda b, i: (b, jnp.maximum(i - 1, 0), 0, 0)
    page = lambda n, idx: pl.BlockSpec((1, 1, n, w), idx)
    rows = lambda idx: pl.BlockSpec((1, w, nk), idx)
    return pl.pallas_call(
        _swa_prompt_kernel,
        grid=(bsz, t // w),
        in_specs=[pl.BlockSpec(memory_space=pltpu.SMEM), page(nq, cur),
                  rows(lambda b, i: (b, jnp.maximum(i - 1, 0), 0)), rows(lambda b, i: (b, i, 0)),
                  page(nk, prev), page(nk, cur),
                  pl.BlockSpec((2 * w, N_HEADS * w), lambda b, i: (0, 0))],
        out_specs=pl.BlockSpec((1, nq, w), lambda b, i: (b, 0, i)),
        out_shape=jax.ShapeDtypeStruct((bsz, nq, t), BF16),
        compiler_params=_params(("parallel", "arbitrary"), 3 * _nbytes((2 * w, N_HEADS * w), F32)),
        name="swa_prompt",
    )(sinks.astype(F32), qt_pages, k, k, vt_pages, vt_pages, rb)


SWA_SAMPLE_ROWS_PER_STEP = 8


def _swa_sample_kernel(sink_ref, qbd_ref, kn_ref, vn_ref, kb_ref, vb_ref, rbb_ref, rbn_ref, o_ref, *, nb, ts):
    cols = qbd_ref.shape[2]
    wbuf = kb_ref.shape[1]
    nr = kn_ref.shape[1]
    sink = sink_ref[...]
    t_b = lax.broadcasted_iota(jnp.int32, (wbuf, cols), 1) % ts
    j_b = lax.broadcasted_iota(jnp.int32, (wbuf, cols), 0)
    dist_b = t_b + wbuf - j_b
    mask_b = (dist_b >= 0) & (dist_b < WINDOW)
    t_n = lax.broadcasted_iota(jnp.int32, (nr, cols), 1) % ts
    s_n = lax.broadcasted_iota(jnp.int32, (nr, cols), 0)
    mask_n = (t_n - s_n >= 0) & (s_n < ts)
    for r in range(nb):
        qbd = qbd_ref[r]
        zb = jnp.dot(kb_ref[r].astype(BF16), qbd, preferred_element_type=F32) + rbb_ref[...]
        zn = jnp.dot(kn_ref[r].astype(BF16), qbd, preferred_element_type=F32) + rbn_ref[...]
        zb = jnp.where(mask_b, zb, NEG_INF)
        zn = jnp.where(mask_n, zn, NEG_INF)
        m = jnp.maximum(jnp.maximum(jnp.max(zb, axis=0, keepdims=True), jnp.max(zn, axis=0, keepdims=True)), sink)
        pb = jnp.exp(zb - m)
        pn = jnp.exp(zn - m)
        denom = jnp.sum(pb, axis=0, keepdims=True) + jnp.sum(pn, axis=0, keepdims=True) + jnp.exp(sink - m)
        inv = 1.0 / denom
        o = (lax.dot_general((pb * inv).astype(BF16), vb_ref[r].astype(BF16), TN_DIMS, preferred_element_type=F32)
             + lax.dot_general((pn * inv).astype(BF16), vn_ref[r].astype(BF16), TN_DIMS, preferred_element_type=F32))
        o_ref[r] = o


def _swa_sample(q, k_new, v_new, k_buf, v_buf, sinks, rel_bias, past_len):
    db, ts, _ = q.shape
    wbuf = k_buf.shape[1]
    nk = N_KV_HEADS * HEAD_DIM
    cols = N_HEADS * ts
    nr = max(SUBLANES, ts)
    nb = SWA_SAMPLE_ROWS_PER_STEP
    q_pos = past_len + jnp.arange(ts)
    k_pos = jnp.concatenate([past_len - wbuf + jnp.arange(wbuf), past_len + jnp.arange(nr)])
    dist = q_pos[:, None] - k_pos[None, :]
    rb = rel_bias[_t5_bucket(dist)].astype(F32)
    rb = jnp.transpose(rb, (1, 2, 0)).reshape(wbuf + nr, cols)
    blk = lambda r, c: pl.BlockSpec((nb, r, c), lambda b: (b, 0, 0))
    full = lambda r, c: pl.BlockSpec((r, c), lambda b: (0, 0))
    o = pl.pallas_call(
        functools.partial(_swa_sample_kernel, nb=nb, ts=ts),
        grid=(db // nb,),
        in_specs=[full(1, cols), blk(nk, cols), blk(nr, nk), blk(nr, nk), blk(wbuf, nk), blk(wbuf, nk),
                  full(wbuf, cols), full(nr, cols)],
        out_specs=blk(cols, nk),
        out_shape=jax.ShapeDtypeStruct((db, cols, nk), F32),
        compiler_params=_params(("parallel",), 2 * nb * _nbytes((wbuf, nk), F32)),
        name="swa_sample",
    )(_col_vector(sinks, ts), _block_diag_queries(q), _pad_rows(k_new, nr), _pad_rows(v_new, nr),
      k_buf.reshape(db, wbuf, nk), v_buf.reshape(db, wbuf, nk), rb[:wbuf], rb[wbuf:])
    return _diag_heads(o, ts).astype(BF16)


def _post_norm(x, branch, gate, g, b, alpha):
    y = alpha * x + gate * branch
    mu = jnp.mean(y, axis=-1, keepdims=True)
    yc = y - mu
    var = jnp.mean(yc * yc, axis=-1, keepdims=True)
    return yc * lax.rsqrt(var + LN_EPS) * g + b


def _wo_norm_kernel(o_ref, x_ref, gate_ref, w_ref, g_ref, b_ref, y_ref, *, alpha, o_transposed):
    dims = TN_DIMS if o_transposed else (((1,), (0,)), ((), ()))
    branch = lax.dot_general(o_ref[0], w_ref[...], dims, preferred_element_type=F32)
    y_ref[0] = _post_norm(x_ref[0], branch, gate_ref[0], g_ref[...], b_ref[...], alpha)


def _wo_norm(o, x, ada, w_bf, ln_g, ln_b, alpha, tb, o_transposed=False):
    bsz, t, d = x.shape
    tm = ada.shape[1]
    nq = w_bf.shape[0]
    blk = lambda n: pl.BlockSpec((1, tb, n), lambda b, i: (b, i, 0))
    o_spec = pl.BlockSpec((1, nq, tb), lambda b, i: (b, 0, i)) if o_transposed else blk(nq)
    vec = pl.BlockSpec((1, d), lambda b, i: (0, 0))
    return pl.pallas_call(
        functools.partial(_wo_norm_kernel, alpha=alpha, o_transposed=o_transposed),
        grid=(bsz, t // tb),
        in_specs=[o_spec, blk(d), _mod_spec(tm, tb, d, 2), pl.BlockSpec(w_bf.shape, lambda b, i: (0, 0)), vec, vec],
        out_specs=blk(d),
        out_shape=jax.ShapeDtypeStruct((bsz, t, d), F32),
        compiler_params=_params(("parallel", "parallel"), _nbytes(w_bf.shape, BF16) + 5 * _nbytes((tb, d), F32)),
        name="wo_norm",
    )(o, x, ada, w_bf, ln_g.reshape(1, d), ln_b.reshape(1, d))


def _topk_ranked(s, iota_k):
    n_keys = s.shape[0]
    pos = jnp.full(s.shape, float(PEER_TOPK), F32)
    vals = []
    for a in range(PEER_TOPK):
        m = jnp.max(s, axis=0, keepdims=True)
        idx = jnp.min(jnp.where(s == m, iota_k, float(n_keys)), axis=0, keepdims=True)
        hit = iota_k == idx
        pos = jnp.where(hit, float(a), pos)
        s = jnp.where(hit, NEG_INF, s)
        vals.append(m)
    return jnp.concatenate(vals, axis=0), pos


def _topk_if_distinct(s):
    pos = jnp.full(s.shape, float(PEER_TOPK), F32)
    vals = []
    for a in range(PEER_TOPK):
        m = jnp.max(s, axis=0, keepdims=True)
        hit = s == m
        pos = jnp.where(hit, float(a), pos)
        s = jnp.where(hit, NEG_INF, s)
        vals.append(m)
    ranked = jnp.sum(jnp.where(pos < float(PEER_TOPK), 1.0, 0.0), axis=0, keepdims=True)
    return jnp.concatenate(vals, axis=0), pos, jnp.max(ranked)


def _candidate_slabs():
    k = PEER_TOPK
    slabs = [("b", 0, 0, 8), ("b", 0, 8, 8)]
    for b in range(1, 8):
        slabs.append(("b", b, 0, k // (b + 1)))
    slabs.append(("a0", None, 0, 8))
    return slabs


def _peer_route_kernel(x_ref, shift_ref, scale_ref, wq_ref, sk_ref, p2_ref, e2_ref, nt_ref, wt_ref,
                       qr_ref, s_ref, t_ref, pos_ref, sel_ref):
    k = PEER_TOPK
    tb = x_ref.shape[1]
    h_t = (x_ref[0] * (1.0 + scale_ref[0]) + shift_ref[0]).T.astype(BF16)
    qr_ref[...] = jnp.dot(wq_ref[...], h_t, preferred_element_type=F32)
    piece = min(tb, LANES)
    iota_k = lax.broadcasted_iota(jnp.int32, (N_KEYS, piece), 0).astype(F32)
    row8 = lax.broadcasted_iota(jnp.int32, (SUBLANES, tb), 0).astype(F32)
    slabs = _candidate_slabs()
    flat = jnp.concatenate(
        [((row8 + a0) * k + b) if kind == "b" else (row8 + 8) for kind, b, a0, _ in slabs], axis=0)
    valid = jnp.concatenate([row8 < nv for _, _, _, nv in slabs], axis=0)
    n_cand = flat.shape[0]
    half = N_KEYS

    def head_body(hd, carry):
        base = pl.multiple_of(hd * 2 * half, 2 * half)
        q1 = qr_ref[pl.ds(base, half), :].astype(BF16)
        q2 = qr_ref[pl.ds(base + half, half), :].astype(BF16)
        s_ref[0] = jnp.dot(sk_ref[0], q1, preferred_element_type=F32)
        s_ref[1] = jnp.dot(sk_ref[1], q2, preferred_element_type=F32)
        pieces_idx = [(lst, slice(c0, c0 + piece)) for lst in range(2) for c0 in range(0, tb, piece)]
        ranked = None
        for lst, lanes in pieces_idx:
            t_fast, pos_fast, n_ranked = _topk_if_distinct(s_ref[lst, :, lanes])
            t_ref[lst, :, lanes] = t_fast
            pos_ref[lst, :, lanes] = pos_fast
            ranked = n_ranked if ranked is None else jnp.maximum(ranked, n_ranked)

        @pl.when(ranked > float(k))
        def _():
            for lst, lanes in pieces_idx:
                t_slow, pos_slow = _topk_ranked(s_ref[lst, :, lanes], iota_k)
                t_ref[lst, :, lanes] = t_slow
                pos_ref[lst, :, lanes] = pos_slow

        s1, s2 = s_ref[0], s_ref[1]
        t1, t2 = t_ref[0], t_ref[1]
        pos1, pos2 = pos_ref[0], pos_ref[1]
        pieces = []
        for kind, b, a0, _ in slabs:
            if kind == "b":
                pieces.append(t1[a0:a0 + 8] + t2[b:b + 1])
            else:
                pieces.append(t1[0:1] + t2[8:16])
        cand = jnp.where(valid, jnp.concatenate(pieces, axis=0), NEG_INF)
        top = t1[0:1] + t2[0:1]
        e = jnp.exp(cand - top)
        sel = jnp.zeros(cand.shape, F32)
        c = cand
        for _ in range(k):
            hit = c == jnp.max(c, axis=0, keepdims=True)
            sel = jnp.where(hit, 1.0, sel)
            c = jnp.where(hit, NEG_INF, c)
        sel_ref[...] = sel

        @pl.when(jnp.max(jnp.sum(sel, axis=0, keepdims=True)) > float(k))
        def _():
            sel = jnp.zeros(cand.shape, F32)
            c = cand
            for _ in range(k):
                m = jnp.max(c, axis=0, keepdims=True)
                idx = jnp.min(jnp.where(c == m, flat, float(k * k)), axis=0, keepdims=True)
                hit = flat == idx
                sel = jnp.where(hit, 1.0, sel)
                c = jnp.where(hit, NEG_INF, c)
            sel_ref[...] = sel

        sel = sel_ref[...]
        z = jnp.sum(sel * e, axis=0, keepdims=True)
        n_lo = sel[0:8]
        for si in range(2, 2 + 7):
            n_lo = n_lo + sel[si * 8:(si + 1) * 8]
        last = jnp.sum(sel[n_cand - 8:], axis=0, keepdims=True)
        n_lo = n_lo + jnp.where(row8 == 0.0, last, 0.0)
        counts = jnp.concatenate([n_lo, sel[8:16]], axis=0)
        nt = jnp.zeros((N_KEYS, tb), F32)
        for a in range(k):
            nt = jnp.where(pos1 == float(a), counts[a:a + 1], nt)
        p2_ref[hd] = pos2.astype(p2_ref.dtype)
        e2_ref[hd] = jnp.exp(s2 - t2[0:1]).astype(e2_ref.dtype)
        nt_ref[hd] = nt
        wt_ref[hd] = jnp.exp(s1 - t1[0:1]) / z
        return carry

    lax.fori_loop(0, PEER_HEADS, head_body, 0)


def _peer_route(x, ada, wq_t, sub_keys_bf, tb):
    bsz, t, d = x.shape
    tm = ada.shape[1]
    n_tok = bsz * t
    nblk = t // tb
    tok = lambda b, i: b * nblk + i
    hk = pl.BlockSpec((PEER_HEADS, N_KEYS, tb), lambda b, i: (0, 0, tok(b, i)))
    return pl.pallas_call(
        _peer_route_kernel,
        grid=(bsz, nblk),
        in_specs=[pl.BlockSpec((1, tb, d), lambda b, i: (b, i, 0)), _mod_spec(tm, tb, d, 3), _mod_spec(tm, tb, d, 4),
                  pl.BlockSpec(wq_t.shape, lambda b, i: (0, 0)),
                  pl.BlockSpec(sub_keys_bf.shape, lambda b, i: (0, 0, 0))],
        out_specs=[hk, hk, hk, hk],
        out_shape=[jax.ShapeDtypeStruct((PEER_HEADS, N_KEYS, n_tok), BF16),
                   jax.ShapeDtypeStruct((PEER_HEADS, N_KEYS, n_tok), BF16),
                   jax.ShapeDtypeStruct((PEER_HEADS, N_KEYS, n_tok), F32),
                   jax.ShapeDtypeStruct((PEER_HEADS, N_KEYS, n_tok), F32)],
        scratch_shapes=[pltpu.VMEM((wq_t.shape[0], tb), F32), pltpu.VMEM((2, N_KEYS, tb), F32),
                        pltpu.VMEM((2, PEER_TOPK, tb), F32), pltpu.VMEM((2, N_KEYS, tb), F32),
                        pltpu.VMEM((SUBLANES * len(_candidate_slabs()), tb), F32)],
        compiler_params=_params(("parallel", "parallel"),
                                _nbytes(wq_t.shape, BF16) + 2 * _nbytes((wq_t.shape[0], tb), F32)
                                + 6 * _nbytes((PEER_HEADS, N_KEYS, tb), F32)),
        name="peer_route",
    )(x, ada, ada, wq_t, sub_keys_bf)


def _gelu_exact(x):
    return 0.5 * x * (1.0 + lax.erf(x * (2.0 ** -0.5)))


def _peer_dense_kernel(x_ref, shift_ref, scale_ref, gate_ref, g_ref, b_ref, p2_ref, e2_ref, nt_ref, wt_ref,
                       u_ref, vt_ref, y_ref, h_ref, a0_ref, a1_ref, p_ref, acc_ref, *, alpha, n_chunks, lane_chunk):
    e = pl.program_id(2)
    n_tiles = pl.num_programs(2) - 1
    tb = h_ref.shape[1]

    @pl.when(e == 0)
    def _():
        h_ref[...] = (x_ref[0] * (1.0 + scale_ref[0]) + shift_ref[0]).T.astype(BF16)
        acc_ref[...] = jnp.zeros_like(acc_ref)
        a1_ref[...] = jnp.zeros_like(a1_ref)

    def bcast_row(ref, hd, row, lanes):
        tile = jnp.broadcast_to(ref[hd, row:row + 1, lanes], (BF16_TILE_ROWS, lane_chunk)).astype(BF16)
        return pltpu.repeat(tile, N_KEYS // BF16_TILE_ROWS, axis=0)

    def step(a_cur, a_nxt):
        for c in range(n_chunks):
            rows = slice(c * N_KEYS, (c + 1) * N_KEYS)
            for j in range(tb // lane_chunk):
                lanes = slice(j * lane_chunk, (j + 1) * lane_chunk)
                gates = None
                for hd in range(PEER_HEADS):
                    n_sel = bcast_row(nt_ref, hd, c, lanes)
                    w = bcast_row(wt_ref, hd, c, lanes)
                    term = jnp.where(p2_ref[hd, :, lanes] < n_sel, e2_ref[hd, :, lanes], jnp.zeros((), BF16)) * w
                    gates = term if gates is None else gates + term
                p_ref[rows, lanes] = _gelu_exact(a_cur[rows, lanes]).astype(BF16) * gates
        a_nxt[...] = jnp.dot(u_ref[...], h_ref[...], preferred_element_type=F32)
        acc_ref[...] += jnp.dot(vt_ref[0], p_ref[...], preferred_element_type=F32)

    parity = lax.rem(e, 2)

    @pl.when(parity == 0)
    def _():
        step(a1_ref, a0_ref)

    @pl.when(parity == 1)
    def _():
        step(a0_ref, a1_ref)

    @pl.when(e == n_tiles)
    def _():
        branch = acc_ref[...].T
        y_ref[0] = _post_norm(x_ref[0], branch, gate_ref[0], g_ref[...], b_ref[...], alpha)


def _peer_dense(x, ada, route, u_bf, vt_bf, ln_g, ln_b, alpha, tb, te):
    bsz, t, d = x.shape
    tm = ada.shape[1]
    nblk = t // tb
    n_exp = u_bf.shape[0]
    n_chunks = te // N_KEYS
    n_tiles = n_exp // te
    p2, e2, nt, wt = route
    tok = lambda b, i: b * nblk + i
    prev = lambda e: jnp.maximum(e - 1, 0)
    hk = pl.BlockSpec((PEER_HEADS, N_KEYS, tb), lambda b, i, e: (0, 0, tok(b, i)))
    hk_tile = pl.BlockSpec((PEER_HEADS, n_chunks, tb), lambda b, i, e: (0, prev(e), tok(b, i)))
    xblk = pl.BlockSpec((1, tb, d), lambda b, i, e: (b, i, 0))
    vec = pl.BlockSpec((1, d), lambda b, i, e: (0, 0))
    return pl.pallas_call(
        functools.partial(_peer_dense_kernel, alpha=alpha, n_chunks=n_chunks, lane_chunk=min(tb, DENSE_LANE_CHUNK)),
        grid=(bsz, nblk, n_tiles + 1),
        in_specs=[xblk, _mod_spec(tm, tb, d, 3), _mod_spec(tm, tb, d, 4), _mod_spec(tm, tb, d, 5), vec, vec,
                  hk, hk, hk_tile, hk_tile,
                  pl.BlockSpec((te, d), lambda b, i, e: (jnp.minimum(e, n_tiles - 1), 0)),
                  pl.BlockSpec((1, d, te), lambda b, i, e: (prev(e), 0, 0))],
        out_specs=xblk,
        out_shape=jax.ShapeDtypeStruct((bsz, t, d), F32),
        scratch_shapes=[pltpu.VMEM((d, tb), BF16), pltpu.VMEM((te, tb), F32), pltpu.VMEM((te, tb), F32),
                        pltpu.VMEM((te, tb), BF16), pltpu.VMEM((d, tb), F32)],
        compiler_params=_params(("parallel", "parallel", "arbitrary"),
                                2 * _nbytes((te, d), BF16) + 2 * _nbytes((tb, d), F32)
                                + 3 * _nbytes((PEER_HEADS, N_KEYS, tb), F32) + 2 * _nbytes((te, tb), F32)),
        name="peer_dense",
    )(x, ada, ada, ada, ln_g.reshape(1, d), ln_b.reshape(1, d), p2, e2, nt, wt, u_bf, vt_bf)


def _token_block(t, want):
    tb = min(t, want)
    while t % tb:
        tb //= 2
    return tb


TOKEN_BLOCK_PROJ = 512
TOKEN_BLOCK_ROUTE = 256
TOKEN_BLOCK_DENSE = 512
EXPERT_TILE = 1024
DENSE_LANE_CHUNK = 512
BF16_TILE_ROWS = 2 * SUBLANES


def kernel(x_prompt, x_sample, c_prompt, c_sample, cache_sb_k, cache_sb_v, page_table, cache_swa_k, cache_swa_v,
           w_ada, b_ada, w_qkv, w_o, attn_sinks, sb_logit_bias, rel_bias, ln_mix_g, ln_mix_b, ln_ffn_g, ln_ffn_b,
           w_peer_q, peer_sub_keys, peer_u, peer_v):
    bsz, t, d = x_prompt.shape
    db, ts, _ = x_sample.shape
    depth = w_ada.shape[0]
    n_pages = page_table.shape[1]
    past_len = n_pages * cache_sb_k.shape[2]
    swa_buf = cache_swa_k.shape[2]
    keep_p = min(WINDOW, t)
    alpha = (2.0 * depth) ** 0.25
    nk = N_KV_HEADS * HEAD_DIM
    n_s = db * ts

    ada_all = _ada_params(jnp.concatenate([c_prompt, c_sample], axis=0), w_ada, b_ada)
    xp = x_prompt
    xs = x_sample.reshape(1, n_s, d)

    tb_p = _token_block(t, TOKEN_BLOCK_PROJ)
    tb_s = _token_block(n_s, TOKEN_BLOCK_PROJ)
    tr_p = _token_block(t, TOKEN_BLOCK_ROUTE)
    tr_s = _token_block(n_s, TOKEN_BLOCK_ROUTE)
    td_p = _token_block(t, TOKEN_BLOCK_DENSE)
    td_s = _token_block(n_s, TOKEN_BLOCK_DENSE)

    sb_kp, sb_vp, sb_ks, sb_vs = [], [], [], []
    swa_kp, swa_vp, swa_ks, swa_vs = [], [], [], []
    for layer in range(depth):
        ada_p = ada_all[layer, :bsz].reshape(bsz, 1, 6 * d)
        ada_s = jnp.repeat(ada_all[layer, bsz:], ts, axis=0).reshape(1, n_s, 6 * d)
        w_qkv_bf = w_qkv[layer].astype(BF16)
        w_o_bf = w_o[layer].astype(BF16)
        wq_t = w_peer_q[layer].T.astype(BF16)
        sub_keys_bf = peer_sub_keys[layer].astype(BF16)
        u_bf = peer_u[layer].astype(BF16)
        vt_bf = jnp.swapaxes(peer_v[layer].reshape(-1, EXPERT_TILE, d), 1, 2).astype(BF16)

        stick_breaking = layer % 2 == 0
        kp, vp, qt_pages, vt_pages = _qkv_proj(xp, ada_p, w_qkv_bf, tb_p, pages=True)
        qs, kn, vn = _qkv_proj(xs, ada_s, w_qkv_bf, tb_s)
        qs3 = qs.reshape(db, ts, N_HEADS * HEAD_DIM)
        kn3 = kn.reshape(db, ts, nk)
        vn3 = vn.reshape(db, ts, nk)
        slot = layer // 2
        if stick_breaking:
            op = _sb_prompt(qt_pages, kp, vt_pages, sb_logit_bias[slot])
            osm = _sb_sample(qs3, kn3, vn3, cache_sb_k, cache_sb_v, slot, page_table, sb_logit_bias[slot])
            sb_kp.append(kp)
            sb_vp.append(vp)
            sb_ks.append(kn3)
            sb_vs.append(vn3)
        else:
            op = _swa_prompt(qt_pages, kp, vt_pages, attn_sinks[slot], rel_bias)
            osm = _swa_sample(qs3, kn3, vn3, cache_swa_k[slot], cache_swa_v[slot], attn_sinks[slot], rel_bias,
                              past_len)
            swa_kp.append(kp[:, t - keep_p:])
            swa_vp.append(vp[:, t - keep_p:])
            ck = cache_swa_k[slot].reshape(db, swa_buf, nk)
            cv = cache_swa_v[slot].reshape(db, swa_buf, nk)
            swa_ks.append(jnp.concatenate([ck, kn3], axis=1)[:, ts:ts + swa_buf])
            swa_vs.append(jnp.concatenate([cv, vn3], axis=1)[:, ts:ts + swa_buf])
        xp = _wo_norm(op, xp, ada_p, w_o_bf, ln_mix_g[layer], ln_mix_b[layer], alpha, tb_p, o_transposed=True)
        xs = _wo_norm(osm.reshape(1, n_s, -1), xs, ada_s, w_o_bf, ln_mix_g[layer], ln_mix_b[layer], alpha, tb_s)

        route_p = _peer_route(xp, ada_p, wq_t, sub_keys_bf, tr_p)
        route_s = _peer_route(xs, ada_s, wq_t, sub_keys_bf, tr_s)
        xp = _peer_dense(xp, ada_p, route_p, u_bf, vt_bf, ln_ffn_g[layer], ln_ffn_b[layer], alpha, td_p, EXPERT_TILE)
        xs = _peer_dense(xs, ada_s, route_s, u_bf, vt_bf, ln_ffn_g[layer], ln_ffn_b[layer], alpha, td_s, EXPERT_TILE)

    def heads(xs_list, lead):
        return jnp.stack(xs_list).reshape(len(xs_list), *lead, N_KV_HEADS, HEAD_DIM)

    return (xp, xs.reshape(db, ts, d),
            heads(sb_kp, (bsz, t)), heads(sb_vp, (bsz, t)), heads(sb_ks, (db, ts)), heads(sb_vs, (db, ts)),
            heads(swa_kp, (bsz, keep_p)), heads(swa_vp, (bsz, keep_p)),
            heads(swa_ks, (db, swa_buf)), heads(swa_vs, (db, swa_buf)))
```
